```python
import jax, jax.numpy as jnp
from jax import lax
import numpy as np

D_MODEL = 1024
BATCH = 2
SEQ = 16384
DEPTH = 4

N_META = 16
DEEPNORM_ALPHA = (2 * DEPTH) ** 0.25
DEEPNORM_BETA = (8 * DEPTH) ** -0.25
LN_EPS = 1e-5
RG_WIDTH = 1344
RG_BLOCKS = 8
RG_BLOCK_W = RG_WIDTH // RG_BLOCKS
RG_CONV_W = 4
RG_C = 8.0
GLA_HEADS = 4
GLA_DK = D_MODEL // 2 // GLA_HEADS
GLA_DV = D_MODEL // GLA_HEADS
GLA_QK = GLA_HEADS * GLA_DK
GLA_VD = GLA_HEADS * GLA_DV
GLA_RANK = 16
GLA_TAU = 16.0
GLA_CHUNK = 64
GLA_PAD = (-N_META) % GLA_CHUNK
D_FF = 3584
N_EXPERTS = 8
TOP_K = 2
N_REC = (DEPTH + 1) // 2
N_GLA = DEPTH // 2

kernel_name = "hybrid_rglru_gla_moe_deepnorm"


def _normalize(x):
    xf = x.astype(jnp.float32)
    mu = jnp.mean(xf, axis=-1, keepdims=True)
    xc = xf - mu
    var = jnp.mean(xc * xc, axis=-1, keepdims=True)
    return xc * lax.rsqrt(var + LN_EPS)


def layer_norm(x, g, b):
    return (_normalize(x) * g + b).astype(x.dtype)


def swiglu(h, w_in, w_out):
    gate, up = jnp.split(h @ w_in, 2, axis=-1)
    return (jax.nn.silu(gate) * up) @ w_out


def _linear_scan_combine(c1, c2):
    a1, b1 = c1
    a2, b2 = c2
    return a1 * a2, a2 * b1 + b2


def rglru_block(h, w_in, conv_w, conv_b, w_gates, b_gates, lam, w_out):
    Bsz, T, _ = h.shape
    proj = h @ w_in
    gate_branch, xr = jnp.split(proj, 2, axis=-1)
    xp = jnp.pad(xr, ((0, 0), (RG_CONV_W - 1, 0), (0, 0)))
    xc = conv_b
    for tap in range(RG_CONV_W):
        xc = xc + xp[:, tap:tap + T] * conv_w[tap]
    gates = jnp.einsum('btnc,gncd->gbtnd', xc.reshape(Bsz, T, RG_BLOCKS, RG_BLOCK_W), w_gates)
    gates = gates.reshape(2, Bsz, T, RG_WIDTH) + b_gates[:, None, None, :]
    gates = jax.nn.sigmoid(gates.astype(jnp.float32))
    r_t, i_t = gates[0], gates[1]
    log_a = -RG_C * r_t * jax.nn.softplus(-lam.astype(jnp.float32))
    a_t = jnp.exp(log_a)
    u_t = jnp.sqrt(-jnp.expm1(2.0 * log_a)) * (i_t * xc)
    _, hs = lax.associative_scan(_linear_scan_combine, (a_t, u_t), axis=1)
    y = hs * jax.nn.gelu(gate_branch.astype(jnp.float32))
    return (y @ w_out).astype(h.dtype)


def gla_block(h, w_in, w_gate_up, b_gate, norm_g, w_out):
    Bsz, T, _ = h.shape
    proj = h @ w_in
    q, k, v, r, z = jnp.split(
        proj, [GLA_QK, 2 * GLA_QK, 2 * GLA_QK + GLA_VD, 2 * GLA_QK + 2 * GLA_VD], axis=-1)
    log_g = jax.nn.log_sigmoid((z @ w_gate_up + b_gate).astype(jnp.float32)) / GLA_TAU
    L = T + GLA_PAD
    nc = L // GLA_CHUNK

    def to_chunks(t, d):
        t = jnp.pad(t, ((0, 0), (GLA_PAD, 0), (0, 0)))
        return t.reshape(Bsz, nc, GLA_CHUNK, GLA_HEADS, d).transpose(0, 3, 1, 2, 4)

    qc = to_chunks(q, GLA_DK) * (GLA_DK ** -0.5)
    kc = to_chunks(k, GLA_DK)
    vc = to_chunks(v, GLA_DV)
    gc = to_chunks(log_g, GLA_DK)
    b = jnp.cumsum(gc, axis=3)
    b_mid = b[:, :, :, GLA_CHUNK // 2 - 1:GLA_CHUNK // 2]
    b_last = b[:, :, :, -1:]
    scores = jnp.einsum('bhncd,bhnsd->bhncs', qc * jnp.exp(b - b_mid), kc * jnp.exp(b_mid - b))
    causal = jnp.tril(jnp.ones((GLA_CHUNK, GLA_CHUNK), dtype=bool))
    scores = jnp.where(causal, scores, 0.0)
    o_intra = jnp.einsum('bhncs,bhnse->bhnce', scores, vc)
    q_in = qc * jnp.exp(b)
    k_out = kc * jnp.exp(b_last - b)
    decay = jnp.exp(b_last[:, :, :, 0])

    def step(S, xs):
        qn, kn, vn, dn = xs
        o = jnp.einsum('bhcd,bhde->bhce', qn, S)
        S = dn[..., None] * S + jnp.einsum('bhcd,bhce->bhde', kn, vn)
        return S, o

    S0 = jnp.zeros((Bsz, GLA_HEADS, GLA_DK, GLA_DV), jnp.float32)
    xs = (jnp.moveaxis(q_in, 2, 0), jnp.moveaxis(k_out, 2, 0),
          jnp.moveaxis(vc, 2, 0), jnp.moveaxis(decay, 2, 0))
    _, o_inter = lax.scan(step, S0, xs)
    o = o_intra + jnp.moveaxis(o_inter, 0, 2)
    o = o.transpose(0, 2, 3, 1, 4).reshape(Bsz, L, GLA_HEADS, GLA_DV)[:, GLA_PAD:]
    o = _normalize(o) * norm_g
    o = o.reshape(Bsz, T, GLA_VD) * jax.nn.silu(r.astype(jnp.float32))
    return (o @ w_out).astype(h.dtype)


def moe_swiglu(h, router, w_in, w_out):
    Bsz, T, D = h.shape
    hf = h.reshape(Bsz * T, D)
    logits = (hf @ router).astype(jnp.float32)
    top_logit, top_idx = lax.top_k(logits, TOP_K)
    top_w = jax.nn.softmax(top_logit, axis=-1)
    gates = jnp.einsum('nk,nke->ne', top_w, jax.nn.one_hot(top_idx, N_EXPERTS, dtype=jnp.float32))
    y = jnp.zeros((Bsz * T, D), jnp.float32)
    for e in range(N_EXPERTS):
        y = y + gates[:, e:e + 1] * swiglu(hf, w_in[e], w_out[e])
    return y.reshape(Bsz, T, D).astype(h.dtype)


def setup_inputs(seed: int = 0) -> dict:
    key = jax.random.key(seed)
    ks = jax.random.split(key, 24)
    f32 = jnp.float32

    def nrm(k, shape, fan_in, scale=1.0):
        return jax.random.normal(k, shape, f32) * (scale * fan_in ** -0.5)

    x = jax.random.normal(ks[0], (BATCH, SEQ, D_MODEL), f32)
    meta_tokens = jax.random.normal(ks[1], (N_META, D_MODEL), f32)
    ln_gain = 1.0 + 0.02 * jax.random.normal(ks[2], (DEPTH, 2, D_MODEL), f32)
    ln_bias = 0.02 * jax.random.normal(ks[3], (DEPTH, 2, D_MODEL), f32)
    rg_w_in = nrm(ks[4], (N_REC, D_MODEL, 2 * RG_WIDTH), D_MODEL)
    rg_conv_w = nrm(ks[5], (N_REC, RG_CONV_W, RG_WIDTH), RG_CONV_W)
    rg_conv_b = 0.02 * jax.random.normal(ks[6], (N_REC, RG_WIDTH), f32)
    rg_w_gates = nrm(ks[7], (N_REC, 2, RG_BLOCKS, RG_BLOCK_W, RG_BLOCK_W), RG_BLOCK_W)
    rg_b_gates = 0.02 * jax.random.normal(ks[8], (N_REC, 2, RG_WIDTH), f32)
    a_pow_c = jax.random.uniform(ks[9], (N_REC, RG_WIDTH), f32, minval=0.9, maxval=0.999)
    a_base = a_pow_c ** (1.0 / RG_C)
    rg_lambda = jnp.log(a_base) - jnp.log1p(-a_base)
    rg_w_out = nrm(ks[10], (N_REC, RG_WIDTH, D_MODEL), RG_WIDTH, DEEPNORM_BETA)
    gla_w_in = nrm(ks[11], (N_GLA, D_MODEL, 2 * GLA_QK + 2 * GLA_VD + GLA_RANK), D_MODEL)
    gla_w_gate_up = nrm(ks[12], (N_GLA, GLA_RANK, GLA_QK), GLA_RANK)
    gla_b_gate = 2.0 + 0.1 * jax.random.normal(ks[13], (N_GLA, GLA_QK), f32)
    gla_norm_g = 1.0 + 0.02 * jax.random.normal(ks[14], (N_GLA, GLA_HEADS, GLA_DV), f32)
    gla_w_out = nrm(ks[15], (N_GLA, GLA_VD, D_MODEL), GLA_VD, DEEPNORM_BETA)
    ffn_w_in = nrm(ks[16], (N_REC, D_MODEL, 2 * D_FF), D_MODEL)
    ffn_w_out = nrm(ks[17], (N_REC, D_FF, D_MODEL), D_FF, DEEPNORM_BETA)
    moe_router = nrm(ks[18], (N_GLA, D_MODEL, N_EXPERTS), D_MODEL)
    moe_w_in = nrm(ks[19], (N_GLA, N_EXPERTS, D_MODEL, 2 * D_FF), D_MODEL)
    moe_w_out = nrm(ks[20], (N_GLA, N_EXPERTS, D_FF, D_MODEL), D_FF, DEEPNORM_BETA)
    return {
        "x": x, "meta_tokens": meta_tokens, "ln_gain": ln_gain, "ln_bias": ln_bias,
        "rg_w_in": rg_w_in, "rg_conv_w": rg_conv_w, "rg_conv_b": rg_conv_b,
        "rg_w_gates": rg_w_gates, "rg_b_gates": rg_b_gates, "rg_lambda": rg_lambda,
        "rg_w_out": rg_w_out,
        "gla_w_in": gla_w_in, "gla_w_gate_up": gla_w_gate_up, "gla_b_gate": gla_b_gate,
        "gla_norm_g": gla_norm_g, "gla_w_out": gla_w_out,
        "ffn_w_in": ffn_w_in, "ffn_w_out": ffn_w_out,
        "moe_router": moe_router, "moe_w_in": moe_w_in, "moe_w_out": moe_w_out,
    }


def reference(x, meta_tokens, ln_gain, ln_bias,
              rg_w_in, rg_conv_w, rg_conv_b, rg_w_gates, rg_b_gates, rg_lambda, rg_w_out,
              gla_w_in, gla_w_gate_up, gla_b_gate, gla_norm_g, gla_w_out,
              ffn_w_in, ffn_w_out, moe_router, moe_w_in, moe_w_out):
    Bsz = x.shape[0]
    meta = jnp.broadcast_to(meta_tokens.astype(x.dtype)[None], (Bsz, N_META, D_MODEL))
    h = jnp.concatenate([meta, x], axis=1)
    for i in range(DEPTH):
        j = i // 2
        if i % 2 == 0:
            mix = rglru_block(h, rg_w_in[j], rg_conv_w[j], rg_conv_b[j], rg_w_gates[j],
                              rg_b_gates[j], rg_lambda[j], rg_w_out[j])
        else:
            mix = gla_block(h, gla_w_in[j], gla_w_gate_up[j], gla_b_gate[j],
                            gla_norm_g[j], gla_w_out[j])
        h = layer_norm(DEEPNORM_ALPHA * h + mix, ln_gain[i, 0], ln_bias[i, 0])
        if i % 2 == 0:
            ff = swiglu(h, ffn_w_in[j], ffn_w_out[j]).astype(h.dtype)
        else:
            ff = moe_swiglu(h, moe_router[j], moe_w_in[j], moe_w_out[j])
        h = layer_norm(DEEPNORM_ALPHA * h + ff, ln_gain[i, 1], ln_bias[i, 1])
    return h[:, N_META:]
```

```python
import functools

import jax
import jax.numpy as jnp
from jax import lax
from jax.experimental import pallas as pl
from jax.experimental.pallas import tpu as pltpu

F32 = jnp.float32
BF16 = jnp.bfloat16

LANES = 128
SUBLANES = 8
VMEM_LIMIT_BYTES = 56 * 1024 * 1024

LN_EPS = 1e-5
RG_C = 8.0
RG_CONV_W = 4
GLA_TAU = 16.0
GLA_CHUNK = 64
TOP_K = 2
GELU_C0 = 0.7978845608028654
GELU_C1 = 0.044715


def _layer_norm(z, g, b):
    mu = jnp.mean(z, axis=-1, keepdims=True)
    zc = z - mu
    var = jnp.mean(zc * zc, axis=-1, keepdims=True)
    return zc * lax.rsqrt(var + LN_EPS) * g + b


def _sigmoid(x):
    return 1.0 / (1.0 + jnp.exp(-x))


def _softplus(x):
    return jnp.maximum(x, 0.0) + jnp.log1p(jnp.exp(-jnp.abs(x)))


def _gelu_tanh(x):
    return 0.5 * x * (1.0 + jnp.tanh(GELU_C0 * (x + GELU_C1 * (x * x * x))))


def _const_spec(shape):
    nd = len(shape)
    return pl.BlockSpec(shape, lambda *_: (0,) * nd)


def _rg_kernel(h_ref, tail0_ref, h0_ref, win_ref, cw_ref, cb_ref, wg_ref, bg_ref,
               lam_ref, wout_ref, lng_ref, lnb_ref,
               out_ref, tail_out_ref, hlast_out_ref,
               xr_scr, gate_scr, xc_scr, rp_scr, ip_scr, y_scr, carry_scr,
               *, tm, wp, ngroups, alpha):
    i = pl.program_id(1)
    gw = wp // ngroups

    @pl.when(i == 0)
    def _():
        xr_scr[0:SUBLANES, :] = tail0_ref[...]
        carry_scr[...] = h0_ref[...]

    @pl.when(i > 0)
    def _():
        xr_scr[0:SUBLANES, :] = xr_scr[tm:tm + SUBLANES, :]

    hb = h_ref[...].astype(BF16)
    gate_scr[...] = jnp.dot(hb, win_ref[:, :wp], preferred_element_type=F32)
    xr_scr[SUBLANES:SUBLANES + tm, :] = jnp.dot(hb, win_ref[:, wp:], preferred_element_type=F32)

    xc = cb_ref[...]
    for tap in range(RG_CONV_W):
        off = SUBLANES - (RG_CONV_W - 1) + tap
        xc = xc + xr_scr[off:off + tm, :] * cw_ref[tap:tap + 1, :]
    xc_scr[...] = xc
    xcb = xc.astype(BF16)
    for g in range(ngroups):
        pre = jnp.dot(xcb[:, g * gw:(g + 1) * gw], wg_ref[g], preferred_element_type=F32)
        rp_scr[:, g * gw:(g + 1) * gw] = pre[:, :gw]
        ip_scr[:, g * gw:(g + 1) * gw] = pre[:, gw:]

    cvec = -RG_C * _softplus(-lam_ref[...])
    row = lax.broadcasted_iota(jnp.int32, (SUBLANES, LANES), 0)

    def body(j, carry):
        r0 = pl.multiple_of(j * SUBLANES, SUBLANES)
        rows = pl.ds(r0, SUBLANES)
        outs = []
        for c in range(wp // LANES):
            cs = slice(c * LANES, (c + 1) * LANES)
            r_t = _sigmoid(rp_scr[rows, cs] + bg_ref[0:1, cs])
            i_t = _sigmoid(ip_scr[rows, cs] + bg_ref[1:2, cs])
            log_a = r_t * cvec[:, cs]
            a_t = jnp.exp(log_a)
            u_t = jnp.sqrt(1.0 - a_t * a_t) * (i_t * xc_scr[rows, cs])
            av, bv = a_t, u_t
            for d in (1, 2, 4):
                m = row >= d
                a_sh = jnp.where(m, pltpu.roll(av, d, 0), 1.0)
                b_sh = jnp.where(m, pltpu.roll(bv, d, 0), 0.0)
                bv = av * b_sh + bv
                av = av * a_sh
            hs = av * carry[:, cs] + bv
            y_scr[rows, cs] = hs * _gelu_tanh(gate_scr[rows, cs])
            outs.append(jnp.broadcast_to(hs[SUBLANES - 1:SUBLANES, :], (SUBLANES, LANES)))
        return jnp.concatenate(outs, axis=1)

    carry = lax.fori_loop(0, tm // SUBLANES, body, carry_scr[...])
    carry_scr[...] = carry

    mix = jnp.dot(y_scr[...].astype(BF16), wout_ref[...], preferred_element_type=F32)
    out_ref[...] = _layer_norm(alpha * h_ref[...] + mix, lng_ref[...], lnb_ref[...])
    tail_out_ref[...] = xr_scr[tm:tm + SUBLANES, :]
    hlast_out_ref[...] = carry


def _rg_layer(h, tail0, h0, p, lng, lnb, *, nb, tm, alpha):
    rows, d = h.shape
    t = rows // nb
    nt = t // tm
    wp = p["conv_b"].shape[1]
    ngroups = p["w_gates"].shape[0]
    kern = functools.partial(_rg_kernel, tm=tm, wp=wp, ngroups=ngroups, alpha=alpha)
    row_spec = pl.BlockSpec((tm, d), lambda b, i: (b * nt + i, 0))
    state_spec = pl.BlockSpec((SUBLANES, wp), lambda b, i: (b, 0))
    return pl.pallas_call(
        kern,
        out_shape=(jax.ShapeDtypeStruct((rows, d), F32),
                   jax.ShapeDtypeStruct((nb * SUBLANES, wp), F32),
                   jax.ShapeDtypeStruct((nb * SUBLANES, wp), F32)),
        grid=(nb, nt),
        in_specs=[row_spec, _const_spec((SUBLANES, wp)), _const_spec((SUBLANES, wp)),
                  _const_spec(p["w_in"].shape), _const_spec(p["conv_w"].shape),
                  _const_spec(p["conv_b"].shape), _const_spec(p["w_gates"].shape),
                  _const_spec(p["b_gates"].shape), _const_spec(p["lam"].shape),
                  _const_spec(p["w_out"].shape), _const_spec(lng.shape), _const_spec(lnb.shape)],
        out_specs=(row_spec, state_spec, state_spec),
        scratch_shapes=[pltpu.VMEM((tm + SUBLANES, wp), F32)] + [pltpu.VMEM((tm, wp), F32)] * 5
                       + [pltpu.VMEM((SUBLANES, wp), F32)],
        compiler_params=pltpu.CompilerParams(
            dimension_semantics=("arbitrary", "arbitrary"), vmem_limit_bytes=VMEM_LIMIT_BYTES),
        name="rg_layer",
    )(h, tail0, h0, p["w_in"], p["conv_w"], p["conv_b"], p["w_gates"], p["b_gates"],
      p["lam"], p["w_out"], lng, lnb)


def _rg_params(w_in, conv_w, conv_b, w_gates, b_gates, lam, w_out):
    width = conv_b.shape[0]
    nblk, bw = w_gates.shape[1], w_gates.shape[2]
    ngroups = nblk // 2
    gw_real = 2 * bw
    gw = -(-gw_real // LANES) * LANES

    def pad_ch(a):
        lead = a.shape[:-1]
        a = a.reshape(lead + (ngroups, gw_real))
        a = jnp.pad(a, [(0, 0)] * len(lead) + [(0, 0), (0, gw - gw_real)])
        return a.reshape(lead + (ngroups * gw,))

    w_in_p = jnp.concatenate([pad_ch(w_in[:, :width]), pad_ch(w_in[:, width:])], axis=1).astype(BF16)
    wg4 = w_gates.reshape(2, ngroups, 2, bw, bw)
    zeros = jnp.zeros((2, ngroups, bw, bw), F32)
    top = jnp.concatenate([wg4[:, :, 0], zeros], axis=-1)
    bot = jnp.concatenate([zeros, wg4[:, :, 1]], axis=-1)
    bd = jnp.concatenate([top, bot], axis=-2)
    bd = jnp.pad(bd, [(0, 0), (0, 0), (0, gw - gw_real), (0, gw - gw_real)])
    w_g = jnp.concatenate([bd[0], bd[1]], axis=-1).astype(BF16)
    w_out_p = pad_ch(w_out.T).T.astype(BF16)
    return {"w_in": w_in_p, "conv_w": pad_ch(conv_w), "conv_b": pad_ch(conv_b)[None],
            "w_gates": w_g, "b_gates": pad_ch(b_gates), "lam": pad_ch(lam)[None], "w_out": w_out_p}


def _gla_kernel(h_ref, s0_ref, win_ref, wgu_ref, bgate_ref, ng_ref, wout_ref, lng_ref, lnb_ref,
                router_ref,
                out_ref, route_ref, s_out_ref,
                q_scr, k_scr, v_scr, r_scr, lg_scr, o_scr, st_scr,
                *, tm, chunk, heads, dk, dv, zp, n_experts, alpha):
    i = pl.program_id(1)
    qk = heads * dk
    vd = heads * dv

    @pl.when(i == 0)
    def _():
        st_scr[...] = s0_ref[...]

    hb = h_ref[...].astype(BF16)
    q_scr[...] = jnp.dot(hb, win_ref[:, 0:qk], preferred_element_type=F32) * (dk ** -0.5)
    k_scr[...] = jnp.dot(hb, win_ref[:, qk:2 * qk], preferred_element_type=F32)
    v_scr[...] = jnp.dot(hb, win_ref[:, 2 * qk:2 * qk + vd], preferred_element_type=F32)
    r_scr[...] = jnp.dot(hb, win_ref[:, 2 * qk + vd:2 * qk + 2 * vd], preferred_element_type=F32)
    z = jnp.dot(hb, win_ref[:, 2 * qk + 2 * vd:2 * qk + 2 * vd + zp], preferred_element_type=F32)
    pre = jnp.dot(z.astype(BF16), wgu_ref[...], preferred_element_type=F32) + bgate_ref[...]
    lg_scr[...] = (jnp.minimum(pre, 0.0) - jnp.log1p(jnp.exp(-jnp.abs(pre)))) * (1.0 / GLA_TAU)

    ri = lax.broadcasted_iota(jnp.int32, (chunk, chunk), 0)
    ci = lax.broadcasted_iota(jnp.int32, (chunk, chunk), 1)
    causal = ri >= ci
    tri = causal.astype(F32)
    mid = chunk // 2 - 1

    def chunk_body(c, _):
        r0 = pl.multiple_of(c * chunk, chunk)
        rows = pl.ds(r0, chunk)
        for hd in range(heads):
            ks = slice(hd * dk, (hd + 1) * dk)
            vs = slice(hd * dv, (hd + 1) * dv)
            bcum = jnp.dot(tri, lg_scr[rows, ks], preferred_element_type=F32,
                           precision=lax.Precision.HIGHEST)
            b_mid = bcum[mid:mid + 1, :]
            b_last = bcum[chunk - 1:chunk, :]
            qv = q_scr[rows, ks]
            kv = k_scr[rows, ks]
            vv = v_scr[rows, vs].astype(BF16)
            q_mid = (qv * jnp.exp(bcum - b_mid)).astype(BF16)
            k_mid = (kv * jnp.exp(b_mid - bcum)).astype(BF16)
            scores = lax.dot_general(q_mid, k_mid, (((1,), (1,)), ((), ())),
                                     preferred_element_type=F32)
            scores = jnp.where(causal, scores, 0.0).astype(BF16)
            o_intra = jnp.dot(scores, vv, preferred_element_type=F32)
            st = st_scr[hd]
            q_in = (qv * jnp.exp(bcum)).astype(BF16)
            o_inter = lax.dot_general(q_in, st.astype(BF16), (((1,), (1,)), ((), ())),
                                      preferred_element_type=F32)
            o_scr[rows, vs] = o_intra + o_inter
            k_out = (kv * jnp.exp(b_last - bcum)).astype(BF16)
            upd = lax.dot_general(vv, k_out, (((0,), (0,)), ((), ())),
                                  preferred_element_type=F32)
            st_scr[hd] = st * jnp.exp(b_last) + upd
        return 0

    lax.fori_loop(0, tm // chunk, chunk_body, 0)

    mix = jnp.zeros((tm, out_ref.shape[1]), F32)
    for hd in range(heads):
        vs = slice(hd * dv, (hd + 1) * dv)
        o = o_scr[:, vs]
        mu = jnp.mean(o, axis=-1, keepdims=True)
        oc = o - mu
        var = jnp.mean(oc * oc, axis=-1, keepdims=True)
        on = oc * lax.rsqrt(var + LN_EPS) * ng_ref[:, vs]
        rv = r_scr[:, vs]
        gated = (on * (rv * _sigmoid(rv))).astype(BF16)
        mix = mix + jnp.dot(gated, wout_ref[vs, :], preferred_element_type=F32)
    h1 = _layer_norm(alpha * h_ref[...] + mix, lng_ref[...], lnb_ref[...])
    out_ref[...] = h1
    s_out_ref[...] = st_scr[...]

    logits = jnp.dot(h1, router_ref[...], preferred_element_type=F32,
                     precision=lax.Precision.HIGHEST)
    lane = lax.broadcasted_iota(jnp.int32, logits.shape, 1)
    neg = jnp.float32(-jnp.inf)
    logits = jnp.where(lane < n_experts, logits, neg)
    top1 = jnp.max(logits, axis=-1, keepdims=True)
    idx1 = jnp.min(jnp.where(logits == top1, lane, LANES), axis=-1, keepdims=True)
    rest = jnp.where(lane == idx1, neg, logits)
    top2 = jnp.max(rest, axis=-1, keepdims=True)
    idx2 = jnp.min(jnp.where(rest == top2, lane, LANES), axis=-1, keepdims=True)
    e2 = jnp.exp(top2 - top1)
    w1 = 1.0 / (1.0 + e2)
    w2 = e2 / (1.0 + e2)
    route = jnp.where(lane == 0, w1, 0.0)
    route = jnp.where(lane == 1, w2, route)
    route = jnp.where(lane == 2, idx1.astype(F32), route)
    route = jnp.where(lane == 3, idx2.astype(F32), route)
    route_ref[...] = route


def _gla_layer(h, s0, p, lng, lnb, router_p, *, nb, tm, chunk, alpha, n_experts):
    rows, d = h.shape
    t = rows // nb
    nt = t // tm
    heads, dv = p["norm_g"].shape
    qk = p["w_gate_up"].shape[1]
    dk = qk // heads
    vd = heads * dv
    zp = p["w_gate_up"].shape[0]
    kern = functools.partial(_gla_kernel, tm=tm, chunk=chunk, heads=heads, dk=dk, dv=dv, zp=zp,
                             n_experts=n_experts, alpha=alpha)
    row_spec = pl.BlockSpec((tm, d), lambda b, i: (b * nt + i, 0))
    ng = p["norm_g"].reshape(1, vd)
    return pl.pallas_call(
        kern,
        out_shape=(jax.ShapeDtypeStruct((rows, d), F32),
                   jax.ShapeDtypeStruct((rows, LANES), F32),
                   jax.ShapeDtypeStruct((nb * heads, dv, dk), F32)),
        grid=(nb, nt),
        in_specs=[row_spec, _const_spec((heads, dv, dk)), _const_spec(p["w_in"].shape),
                  _const_spec(p["w_gate_up"].shape), _const_spec(p["b_gate"].shape),
                  _const_spec(ng.shape), _const_spec(p["w_out"].shape),
                  _const_spec(lng.shape), _const_spec(lnb.shape), _const_spec(router_p.shape)],
        out_specs=(row_spec,
                   pl.BlockSpec((tm, LANES), lambda b, i: (b * nt + i, 0)),
                   pl.BlockSpec((heads, dv, dk), lambda b, i: (b, 0, 0))),
        scratch_shapes=[pltpu.VMEM((tm, qk), F32), pltpu.VMEM((tm, qk), F32),
                        pltpu.VMEM((tm, vd), F32), pltpu.VMEM((tm, vd), F32),
                        pltpu.VMEM((tm, qk), F32), pltpu.VMEM((tm, vd), F32),
                        pltpu.VMEM((heads, dv, dk), F32)],
        compiler_params=pltpu.CompilerParams(
            dimension_semantics=("arbitrary", "arbitrary"), vmem_limit_bytes=VMEM_LIMIT_BYTES),
        name="gla_layer",
    )(h, s0, p["w_in"], p["w_gate_up"], p["b_gate"], ng, p["w_out"], lng, lnb, router_p)


def _gla_params(w_in, w_gate_up, b_gate, norm_g, w_out):
    rank, qk = w_gate_up.shape
    zp = -(-rank // LANES) * LANES
    w_in_p = jnp.pad(w_in, ((0, 0), (0, zp - rank))).astype(BF16)
    wgu = jnp.pad(w_gate_up, ((0, zp - rank), (0, 0))).astype(BF16)
    return {"w_in": w_in_p, "w_gate_up": wgu, "b_gate": b_gate[None], "norm_g": norm_g,
            "w_out": w_out.astype(BF16)}


def _ffn_kernel(x_ref, win_ref, wout_ref, lng_ref, lnb_ref, out_ref, *, tf, alpha):
    x = x_ref[...]
    xb = x.astype(BF16)
    dff = wout_ref.shape[0]
    acc = jnp.zeros(x.shape, F32)
    for f in range(dff // tf):
        g = jnp.dot(xb, win_ref[:, f * tf:(f + 1) * tf], preferred_element_type=F32)
        u = jnp.dot(xb, win_ref[:, dff + f * tf:dff + (f + 1) * tf], preferred_element_type=F32)
        a = ((g * _sigmoid(g)) * u).astype(BF16)
        acc = acc + jnp.dot(a, wout_ref[f * tf:(f + 1) * tf, :], preferred_element_type=F32)
    out_ref[...] = _layer_norm(alpha * x + acc, lng_ref[...], lnb_ref[...])


def _ffn_layer(x, w_in, w_out, lng, lnb, *, tm, tf, alpha):
    rows, d = x.shape
    kern = functools.partial(_ffn_kernel, tf=tf, alpha=alpha)
    row_spec = pl.BlockSpec((tm, d), lambda i: (i, 0))
    return pl.pallas_call(
        kern,
        out_shape=jax.ShapeDtypeStruct((rows, d), F32),
        grid=(rows // tm,),
        in_specs=[row_spec,
                  pl.BlockSpec(w_in.shape, lambda i: (0, 0), pipeline_mode=pl.Buffered(1)),
                  pl.BlockSpec(w_out.shape, lambda i: (0, 0), pipeline_mode=pl.Buffered(1)),
                  _const_spec(lng.shape), _const_spec(lnb.shape)],
        out_specs=row_spec,
        compiler_params=pltpu.CompilerParams(
            dimension_semantics=("arbitrary",), vmem_limit_bytes=VMEM_LIMIT_BYTES),
        name="ffn_layer",
    )(x, w_in, w_out, lng, lnb)


def _moe_kernel(te_ref, tc_ref, pid_ref, x_hbm, wg_ref, wu_ref, wo_ref, y_hbm,
                xbuf, acc, sem_g, sem_s, *, tm, n_tokens):
    j = pl.program_id(0)
    f = pl.program_id(1)
    nf = pl.num_programs(1)
    cnt = tc_ref[j]

    def token_of(r):
        p = pid_ref[0, 0, r]
        return p, jnp.where(p >= n_tokens, p - n_tokens, p)

    def gather_copy(r):
        _, tok = token_of(r)
        return pltpu.make_async_copy(x_hbm.at[pl.ds(tok, 1), :], xbuf.at[pl.ds(r, 1), :], sem_g)

    def scatter_copy(r):
        p, _ = token_of(r)
        return pltpu.make_async_copy(acc.at[pl.ds(r, 1), :], y_hbm.at[pl.ds(p, 1), :], sem_s)

    @pl.when(jnp.logical_and(j == 0, f == 0))
    def _():
        xbuf[...] = jnp.zeros(xbuf.shape, xbuf.dtype)

    @pl.when(jnp.logical_and(f == 0, cnt > 0))
    def _():
        def start(r, c):
            gather_copy(r).start()
            return c
        lax.fori_loop(0, cnt, start, 0)

        def wait(r, c):
            gather_copy(r).wait()
            return c
        lax.fori_loop(0, cnt, wait, 0)

    @pl.when(cnt > 0)
    def _():
        xb = xbuf[...].astype(BF16)
        g = jnp.dot(xb, wg_ref[...], preferred_element_type=F32)
        u = jnp.dot(xb, wu_ref[...], preferred_element_type=F32)
        a = ((g * _sigmoid(g)) * u).astype(BF16)
        part = jnp.dot(a, wo_ref[...], preferred_element_type=F32)

        @pl.when(f == 0)
        def _():
            acc[...] = part

        @pl.when(f > 0)
        def _():
            acc[...] = acc[...] + part

    @pl.when(jnp.logical_and(f == nf - 1, cnt > 0))
    def _():
        def start(r, c):
            scatter_copy(r).start()
            return c
        lax.fori_loop(0, cnt, start, 0)

        def wait(r, c):
            scatter_copy(r).wait()
            return c
        lax.fori_loop(0, cnt, wait, 0)


def _moe_experts(h1, pair_ids, tile_e, tile_cnt, w_in, w_out, *, tm, tf):
    n, d = h1.shape
    n_exp, _, two_f = w_in.shape
    dff = two_f // 2
    nf = dff // tf
    nt = pair_ids.shape[0]
    kern = functools.partial(_moe_kernel, tm=tm, n_tokens=n)

    def f_eff(j, f, te, tc):
        return jnp.where(tc[j] > 0, f, nf - 1)

    grid_spec = pltpu.PrefetchScalarGridSpec(
        num_scalar_prefetch=2,
        grid=(nt, nf),
        in_specs=[
            pl.BlockSpec((1, 1, tm), lambda j, f, te, tc: (j, 0, 0), memory_space=pltpu.SMEM),
            pl.BlockSpec(memory_space=pl.ANY),
            pl.BlockSpec((None, d, tf), lambda j, f, te, tc: (te[j], 0, f_eff(j, f, te, tc))),
            pl.BlockSpec((None, d, tf), lambda j, f, te, tc: (te[j], 0, nf + f_eff(j, f, te, tc))),
            pl.BlockSpec((None, tf, d), lambda j, f, te, tc: (te[j], f_eff(j, f, te, tc), 0)),
        ],
        out_specs=pl.BlockSpec(memory_space=pl.ANY),
        scratch_shapes=[pltpu.VMEM((tm, d), F32), pltpu.VMEM((tm, d), F32),
                        pltpu.SemaphoreType.DMA, pltpu.SemaphoreType.DMA],
    )
    return pl.pallas_call(
        kern,
        out_shape=jax.ShapeDtypeStruct((TOP_K * n, d), F32),
        grid_spec=grid_spec,
        compiler_params=pltpu.CompilerParams(
            dimension_semantics=("arbitrary", "arbitrary"), vmem_limit_bytes=VMEM_LIMIT_BYTES),
        name="moe_experts",
    )(tile_e, tile_cnt, pair_ids, h1, w_in, w_in, w_out)


def _moe_schedule(route, n_experts, tm):
    n = route.shape[0]
    e_flat = jnp.concatenate([route[:, 2], route[:, 3]]).astype(jnp.int32)
    order = jnp.argsort(e_flat, stable=True).astype(jnp.int32)
    counts = jnp.sum(e_flat[:, None] == jnp.arange(n_experts, dtype=jnp.int32)[None, :],
                     axis=0, dtype=jnp.int32)
    padded = -(-counts // tm) * tm
    pend = jnp.cumsum(padded)
    pstart = pend - padded
    ustart = jnp.cumsum(counts) - counts
    nt = (TOP_K * n) // tm + n_experts
    tile_start = jnp.arange(nt, dtype=jnp.int32) * tm
    tile_e = jnp.minimum(jnp.searchsorted(pend, tile_start, side="right"), n_experts - 1).astype(jnp.int32)
    in_group = tile_start - pstart[tile_e]
    tile_cnt = jnp.where(tile_start < pend[-1], jnp.clip(counts[tile_e] - in_group, 0, tm), 0).astype(jnp.int32)
    src = (ustart[tile_e] + in_group)[:, None] + jnp.arange(tm, dtype=jnp.int32)[None, :]
    pair_ids = order[jnp.clip(src, 0, TOP_K * n - 1)]
    return pair_ids.reshape(nt, 1, tm), tile_e, tile_cnt


def _combine_kernel(h_ref, y0_ref, y1_ref, route_ref, lng_ref, lnb_ref, out_ref, *, alpha):
    route = route_ref[...]
    ff = route[:, 0:1] * y0_ref[...] + route[:, 1:2] * y1_ref[...]
    out_ref[...] = _layer_norm(alpha * h_ref[...] + ff, lng_ref[...], lnb_ref[...])


def _moe_combine(h1, y2, route, lng, lnb, *, tm, alpha):
    n, d = h1.shape
    nt = n // tm
    kern = functools.partial(_combine_kernel, alpha=alpha)
    row_spec = pl.BlockSpec((tm, d), lambda i: (i, 0))
    return pl.pallas_call(
        kern,
        out_shape=jax.ShapeDtypeStruct((n, d), F32),
        grid=(nt,),
        in_specs=[row_spec, row_spec, pl.BlockSpec((tm, d), lambda i: (nt + i, 0)),
                  pl.BlockSpec((tm, LANES), lambda i: (i, 0)),
                  _const_spec(lng.shape), _const_spec(lnb.shape)],
        out_specs=row_spec,
        compiler_params=pltpu.CompilerParams(
            dimension_semantics=("arbitrary",), vmem_limit_bytes=VMEM_LIMIT_BYTES),
        name="moe_combine",
    )(h1, y2, y2, route, lng, lnb)


def _moe_layer(h1, route, w_in, w_out, lng, lnb, *, tm_moe, tf, tm_row, alpha):
    n_experts = w_in.shape[0]
    pair_ids, tile_e, tile_cnt = _moe_schedule(route, n_experts, tm_moe)
    y2 = _moe_experts(h1, pair_ids, tile_e, tile_cnt, w_in, w_out, tm=tm_moe, tf=tf)
    return _moe_combine(h1, y2, route, lng, lnb, tm=tm_row, alpha=alpha)


def _row_tile(rows, pref):
    tm = min(pref, rows)
    while rows % tm:
        tm //= 2
    return tm


def kernel(x, meta_tokens, ln_gain, ln_bias, rg_w_in, rg_conv_w, rg_conv_b, rg_w_gates, rg_b_gates, rg_lambda, rg_w_out, gla_w_in, gla_w_gate_up, gla_b_gate, gla_norm_g, gla_w_out, ffn_w_in, ffn_w_out, moe_router, moe_w_in, moe_w_out):
    bsz, seq, d = x.shape
    n_meta = meta_tokens.shape[0]
    depth = ln_gain.shape[0]
    alpha = (2.0 * depth) ** 0.25
    n_experts = moe_router.shape[-1]
    heads, dv = gla_norm_g.shape[1:]
    dk = gla_w_gate_up.shape[-1] // heads

    hm = x.reshape(bsz * seq, d)
    hq = meta_tokens.astype(x.dtype)

    tm_rg = _row_tile(seq, 256)
    tm_gla = _row_tile(seq, 256)
    tm_ffn = _row_tile(bsz * seq, 512)
    tm_moe = _row_tile(TOP_K * bsz * seq, 1024)
    tf = 512

    for i in range(depth):
        j = i // 2
        lng0, lnb0 = ln_gain[i, 0][None], ln_bias[i, 0][None]
        lng1, lnb1 = ln_gain[i, 1][None], ln_bias[i, 1][None]
        last = i == depth - 1
        if i % 2 == 0:
            p = _rg_params(rg_w_in[j], rg_conv_w[j], rg_conv_b[j], rg_w_gates[j], rg_b_gates[j],
                           rg_lambda[j], rg_w_out[j])
            wp = p["conv_b"].shape[1]
            zero_state = jnp.zeros((SUBLANES, wp), F32)
            hq, tail, hlast = _rg_layer(hq, zero_state, zero_state, p, lng0, lnb0,
                                        nb=1, tm=n_meta, alpha=alpha)
            hm, _, _ = _rg_layer(hm, tail, hlast, p, lng0, lnb0, nb=bsz, tm=tm_rg, alpha=alpha)
            w_in = ffn_w_in[j].astype(BF16)
            w_out = ffn_w_out[j].astype(BF16)
            if not last:
                hq = _ffn_layer(hq, w_in, w_out, lng1, lnb1, tm=n_meta, tf=tf, alpha=alpha)
            hm = _ffn_layer(hm, w_in, w_out, lng1, lnb1, tm=tm_ffn, tf=tf, alpha=alpha)
        else:
            p = _gla_params(gla_w_in[j], gla_w_gate_up[j], gla_b_gate[j], gla_norm_g[j], gla_w_out[j])
            router_p = jnp.pad(moe_router[j], ((0, 0), (0, LANES - n_experts)))
            s0 = jnp.zeros((heads, dv, dk), F32)
            hq, route_q, s_meta = _gla_layer(hq, s0, p, lng0, lnb0, router_p, nb=1, tm=n_meta,
                                             chunk=n_meta, alpha=alpha, n_experts=n_experts)
            hm, route_m, _ = _gla_layer(hm, s_meta, p, lng0, lnb0, router_p, nb=bsz, tm=tm_gla,
                                        chunk=GLA_CHUNK, alpha=alpha, n_experts=n_experts)
            w_in = moe_w_in[j].astype(BF16)
            w_out = moe_w_out[j].astype(BF16)
            if not last:
                hq = _moe_layer(hq, route_q, w_in, w_out, lng1, lnb1, tm_moe=n_meta, tf=tf,
                                tm_row=n_meta, alpha=alpha)
            hm = _moe_layer(hm, route_m, w_in, w_out, lng1, lnb1, tm_moe=tm_moe, tf=tf,
                            tm_row=tm_ffn, alpha=alpha)
    return hm.reshape(bsz, seq, d)
```

```python
import functools

import jax
import jax.numpy as jnp
from jax import lax
from jax.experimental import pallas as pl
from jax.experimental.pallas import tpu as pltpu

F32 = jnp.float32
BF16 = jnp.bfloat16

LANES = 128
SUBLANES = 8
VMEM_LIMIT_BYTES = 56 * 1024 * 1024

LN_EPS = 1e-5
RG_C = 8.0
RG_CONV_W = 4
GLA_TAU = 16.0
GLA_CHUNK = 64
TOP_K = 2
GELU_C0 = 0.7978845608028654
GELU_C1 = 0.044715


def _layer_norm(z, g, b):
    mu = jnp.mean(z, axis=-1, keepdims=True)
    zc = z - mu
    var = jnp.mean(zc * zc, axis=-1, keepdims=True)
    return zc * lax.rsqrt(var + LN_EPS) * g + b


def _sigmoid(x):
    return 1.0 / (1.0 + jnp.exp(-x))


def _softplus(x):
    return jnp.maximum(x, 0.0) + jnp.log1p(jnp.exp(-jnp.abs(x)))


def _gelu_tanh(x):
    return 0.5 * x * (1.0 + jnp.tanh(GELU_C0 * (x + GELU_C1 * (x * x * x))))


def _const_spec(shape):
    nd = len(shape)
    return pl.BlockSpec(shape, lambda *_: (0,) * nd)


def _rg_kernel(h_ref, tail0_ref, h0_ref, win_ref, cw_ref, cb_ref, wg_ref, bg_ref,
               lam_ref, wout_ref, lng_ref, lnb_ref,
               out_ref, tail_out_ref, hlast_out_ref,
               xr_scr, gate_scr, xc_scr, rp_scr, ip_scr, y_scr, carry_scr,
               *, tm, wp, ngroups, alpha):
    i = pl.program_id(1)
    gw = wp // ngroups

    @pl.when(i == 0)
    def _():
        xr_scr[0:SUBLANES, :] = tail0_ref[...]
        carry_scr[...] = h0_ref[...]

    @pl.when(i > 0)
    def _():
        xr_scr[0:SUBLANES, :] = xr_scr[tm:tm + SUBLANES, :]

    hb = h_ref[...].astype(BF16)
    gate_scr[...] = jnp.dot(hb, win_ref[:, :wp], preferred_element_type=F32)
    xr_scr[SUBLANES:SUBLANES + tm, :] = jnp.dot(hb, win_ref[:, wp:], preferred_element_type=F32)

    xc = cb_ref[...]
    for tap in range(RG_CONV_W):
        off = SUBLANES - (RG_CONV_W - 1) + tap
        xc = xc + xr_scr[off:off + tm, :] * cw_ref[tap:tap + 1, :]
    xc_scr[...] = xc
    xcb = xc.astype(BF16)
    for g in range(ngroups):
        pre = jnp.dot(xcb[:, g * gw:(g + 1) * gw], wg_ref[g], preferred_element_type=F32)
        rp_scr[:, g * gw:(g + 1) * gw] = pre[:, :gw]
        ip_scr[:, g * gw:(g + 1) * gw] = pre[:, gw:]

    cvec = -RG_C * _softplus(-lam_ref[...])
    row = lax.broadcasted_iota(jnp.int32, (SUBLANES, LANES), 0)

    def body(j, carry):
        r0 = pl.multiple_of(j * SUBLANES, SUBLANES)
        rows = pl.ds(r0, SUBLANES)
        outs = []
        for c in range(wp // LANES):
            cs = slice(c * LANES, (c + 1) * LANES)
            r_t = _sigmoid(rp_scr[rows, cs] + bg_ref[0:1, cs])
            i_t = _sigmoid(ip_scr[rows, cs] + bg_ref[1:2, cs])
            log_a = r_t * cvec[:, cs]
            a_t = jnp.exp(log_a)
            u_t = jnp.sqrt(1.0 - a_t * a_t) * (i_t * xc_scr[rows, cs])
            av, bv = a_t, u_t
            for d in (1, 2, 4):
                m = row >= d
                a_sh = jnp.where(m, pltpu.roll(av, d, 0), 1.0)
                b_sh = jnp.where(m, pltpu.roll(bv, d, 0), 0.0)
                bv = av * b_sh + bv
                av = av * a_sh
            hs = av * carry[:, cs] + bv
            y_scr[rows, cs] = hs * _gelu_tanh(gate_scr[rows, cs])
            outs.append(jnp.broadcast_to(hs[SUBLANES - 1:SUBLANES, :], (SUBLANES, LANES)))
        return jnp.concatenate(outs, axis=1)

    carry = lax.fori_loop(0, tm // SUBLANES, body, carry_scr[...])
    carry_scr[...] = carry

    mix = jnp.dot(y_scr[...].astype(BF16), wout_ref[...], preferred_element_type=F32)
    out_ref[...] = _layer_norm(alpha * h_ref[...] + mix, lng_ref[...], lnb_ref[...])
    tail_out_ref[...] = xr_scr[tm:tm + SUBLANES, :]
    hlast_out_ref[...] = carry


def _rg_layer(h, tail0, h0, p, lng, lnb, *, nb, tm, alpha):
    rows, d = h.shape
    t = rows // nb
    nt = t // tm
    wp = p["conv_b"].shape[1]
    ngroups = p["w_gates"].shape[0]
    kern = functools.partial(_rg_kernel, tm=tm, wp=wp, ngroups=ngroups, alpha=alpha)
    row_spec = pl.BlockSpec((tm, d), lambda b, i: (b * nt + i, 0))
    state_spec = pl.BlockSpec((SUBLANES, wp), lambda b, i: (b, 0))
    return pl.pallas_call(
        kern,
        out_shape=(jax.ShapeDtypeStruct((rows, d), F32),
                   jax.ShapeDtypeStruct((nb * SUBLANES, wp), F32),
                   jax.ShapeDtypeStruct((nb * SUBLANES, wp), F32)),
        grid=(nb, nt),
        in_specs=[row_spec, _const_spec((SUBLANES, wp)), _const_spec((SUBLANES, wp)),
                  _const_spec(p["w_in"].shape), _const_spec(p["conv_w"].shape),
                  _const_spec(p["conv_b"].shape), _const_spec(p["w_gates"].shape),
                  _const_spec(p["b_gates"].shape), _const_spec(p["lam"].shape),
                  _const_spec(p["w_out"].shape), _const_spec(lng.shape), _const_spec(lnb.shape)],
        out_specs=(row_spec, state_spec, state_spec),
        scratch_shapes=[pltpu.VMEM((tm + SUBLANES, wp), F32)] + [pltpu.VMEM((tm, wp), F32)] * 5
                       + [pltpu.VMEM((SUBLANES, wp), F32)],
        compiler_params=pltpu.CompilerParams(
            dimension_semantics=("arbitrary", "arbitrary"), vmem_limit_bytes=VMEM_LIMIT_BYTES),
        name="rg_layer",
    )(h, tail0, h0, p["w_in"], p["conv_w"], p["conv_b"], p["w_gates"], p["b_gates"],
      p["lam"], p["w_out"], lng, lnb)


def _rg_params(w_in, conv_w, conv_b, w_gates, b_gates, lam, w_out):
    width = conv_b.shape[0]
    nblk, bw = w_gates.shape[1], w_gates.shape[2]
    ngroups = nblk // 2
    gw_real = 2 * bw
    gw = -(-gw_real // LANES) * LANES

    def pad_ch(a):
        lead = a.shape[:-1]
        a = a.reshape(lead + (ngroups, gw_real))
        a = jnp.pad(a, [(0, 0)] * len(lead) + [(0, 0), (0, gw - gw_real)])
        return a.reshape(lead + (ngroups * gw,))

    w_in_p = jnp.concatenate([pad_ch(w_in[:, :width]), pad_ch(w_in[:, width:])], axis=1).astype(BF16)
    wg4 = w_gates.reshape(2, ngroups, 2, bw, bw)
    zeros = jnp.zeros((2, ngroups, bw, bw), F32)
    top = jnp.concatenate([wg4[:, :, 0], zeros], axis=-1)
    bot = jnp.concatenate([zeros, wg4[:, :, 1]], axis=-1)
    bd = jnp.concatenate([top, bot], axis=-2)
    bd = jnp.pad(bd, [(0, 0), (0, 0), (0, gw - gw_real), (0, gw - gw_real)])
    w_g = jnp.concatenate([bd[0], bd[1]], axis=-1).astype(BF16)
    w_out_p = pad_ch(w_out.T).T.astype(BF16)
    return {"w_in": w_in_p, "conv_w": pad_ch(conv_w), "conv_b": pad_ch(conv_b)[None],
            "w_gates": w_g, "b_gates": pad_ch(b_gates), "lam": pad_ch(lam)[None], "w_out": w_out_p}


def _gla_kernel(h_ref, s0_ref, win_ref, wgu_ref, bgate_ref, ng_ref, wout_ref, lng_ref, lnb_ref,
                router_ref,
                out_ref, route_ref, s_out_ref,
                q_scr, k_scr, v_scr, r_scr, lg_scr, o_scr, st_scr,
                *, tm, chunk, heads, dk, dv, zp, n_experts, alpha):
    i = pl.program_id(1)
    qk = heads * dk
    vd = heads * dv

    @pl.when(i == 0)
    def _():
        st_scr[...] = s0_ref[...]

    hb = h_ref[...].astype(BF16)
    q_scr[...] = jnp.dot(hb, win_ref[:, 0:qk], preferred_element_type=F32) * (dk ** -0.5)
    k_scr[...] = jnp.dot(hb, win_ref[:, qk:2 * qk], preferred_element_type=F32)
    v_scr[...] = jnp.dot(hb, win_ref[:, 2 * qk:2 * qk + vd], preferred_element_type=F32)
    r_scr[...] = jnp.dot(hb, win_ref[:, 2 * qk + vd:2 * qk + 2 * vd], preferred_element_type=F32)
    z = jnp.dot(hb, win_ref[:, 2 * qk + 2 * vd:2 * qk + 2 * vd + zp], preferred_element_type=F32)
    pre = jnp.dot(z.astype(BF16), wgu_ref[...], preferred_element_type=F32) + bgate_ref[...]
    lg_scr[...] = (jnp.minimum(pre, 0.0) - jnp.log1p(jnp.exp(-jnp.abs(pre)))) * (1.0 / GLA_TAU)

    ri = lax.broadcasted_iota(jnp.int32, (chunk, chunk), 0)
    ci = lax.broadcasted_iota(jnp.int32, (chunk, chunk), 1)
    causal = ri >= ci
    tri = causal.astype(F32)
    mid = chunk // 2 - 1

    def chunk_body(c, _):
        r0 = pl.multiple_of(c * chunk, chunk)
        rows = pl.ds(r0, chunk)
        for hd in range(heads):
            ks = slice(hd * dk, (hd + 1) * dk)
            vs = slice(hd * dv, (hd + 1) * dv)
            bcum = jnp.dot(tri, lg_scr[rows, ks], preferred_element_type=F32,
                           precision=lax.Precision.HIGHEST)
            b_mid = bcum[mid:mid + 1, :]
            b_last = bcum[chunk - 1:chunk, :]
            qv = q_scr[rows, ks]
            kv = k_scr[rows, ks]
            vv = v_scr[rows, vs].astype(BF16)
            q_mid = (qv * jnp.exp(bcum - b_mid)).astype(BF16)
            k_mid = (kv * jnp.exp(b_mid - bcum)).astype(BF16)
            scores = lax.dot_general(q_mid, k_mid, (((1,), (1,)), ((), ())),
                                     preferred_element_type=F32)
            scores = jnp.where(causal, scores, 0.0).astype(BF16)
            o_intra = jnp.dot(scores, vv, preferred_element_type=F32)
            st = st_scr[hd]
            q_in = (qv * jnp.exp(bcum)).astype(BF16)
            o_inter = lax.dot_general(q_in, st.astype(BF16), (((1,), (1,)), ((), ())),
                                      preferred_element_type=F32)
            o_scr[rows, vs] = o_intra + o_inter
            k_out = (kv * jnp.exp(b_last - bcum)).astype(BF16)
            upd = lax.dot_general(vv, k_out, (((0,), (0,)), ((), ())),
                                  preferred_element_type=F32)
            st_scr[hd] = st * jnp.exp(b_last) + upd
        return 0

    lax.fori_loop(0, tm // chunk, chunk_body, 0)

    mix = jnp.zeros((tm, out_ref.shape[1]), F32)
    for hd in range(heads):
        vs = slice(hd * dv, (hd + 1) * dv)
        o = o_scr[:, vs]
        mu = jnp.mean(o, axis=-1, keepdims=True)
        oc = o - mu
        var = jnp.mean(oc * oc, axis=-1, keepdims=True)
        on = oc * lax.rsqrt(var + LN_EPS) * ng_ref[:, vs]
        rv = r_scr[:, vs]
        gated = (on * (rv * _sigmoid(rv))).astype(BF16)
        mix = mix + jnp.dot(gated, wout_ref[vs, :], preferred_element_type=F32)
    h1 = _layer_norm(alpha * h_ref[...] + mix, lng_ref[...], lnb_ref[...])
    out_ref[...] = h1
    s_out_ref[...] = st_scr[...]

    logits = jnp.dot(h1, router_ref[...], preferred_element_type=F32,
                     precision=lax.Precision.HIGHEST)
    lane = lax.broadcasted_iota(jnp.int32, logits.shape, 1)
    neg = jnp.float32(-jnp.inf)
    logits = jnp.where(lane < n_experts, logits, neg)
    top1 = jnp.max(logits, axis=-1, keepdims=True)
    idx1 = jnp.min(jnp.where(logits == top1, lane, LANES), axis=-1, keepdims=True)
    rest = jnp.where(lane == idx1, neg, logits)
    top2 = jnp.max(rest, axis=-1, keepdims=True)
    idx2 = jnp.min(jnp.where(rest == top2, lane, LANES), axis=-1, keepdims=True)
    e2 = jnp.exp(top2 - top1)
    w1 = 1.0 / (1.0 + e2)
    w2 = e2 / (1.0 + e2)
    route = jnp.where(lane == 0, w1, 0.0)
    route = jnp.where(lane == 1, w2, route)
    route = jnp.where(lane == 2, idx1.astype(F32), route)
    route = jnp.where(lane == 3, idx2.astype(F32), route)
    route_ref[...] = route


def _gla_layer(h, s0, p, lng, lnb, router_p, *, nb, tm, chunk, alpha, n_experts):
    rows, d = h.shape
    t = rows // nb
    nt = t // tm
    heads, dv = p["norm_g"].shape
    qk = p["w_gate_up"].shape[1]
    dk = qk // heads
    vd = heads * dv
    zp = p["w_gate_up"].shape[0]
    kern = functools.partial(_gla_kernel, tm=tm, chunk=chunk, heads=heads, dk=dk, dv=dv, zp=zp,
                             n_experts=n_experts, alpha=alpha)
    row_spec = pl.BlockSpec((tm, d), lambda b, i: (b * nt + i, 0))
    ng = p["norm_g"].reshape(1, vd)
    return pl.pallas_call(
        kern,
        out_shape=(jax.ShapeDtypeStruct((rows, d), F32),
                   jax.ShapeDtypeStruct((rows, LANES), F32),
                   jax.ShapeDtypeStruct((nb * heads, dv, dk), F32)),
        grid=(nb, nt),
        in_specs=[row_spec, _const_spec((heads, dv, dk)), _const_spec(p["w_in"].shape),
                  _const_spec(p["w_gate_up"].shape), _const_spec(p["b_gate"].shape),
                  _const_spec(ng.shape), _const_spec(p["w_out"].shape),
                  _const_spec(lng.shape), _const_spec(lnb.shape), _const_spec(router_p.shape)],
        out_specs=(row_spec,
                   pl.BlockSpec((tm, LANES), lambda b, i: (b * nt + i, 0)),
                   pl.BlockSpec((heads, dv, dk), lambda b, i: (b, 0, 0))),
        scratch_shapes=[pltpu.VMEM((tm, qk), F32), pltpu.VMEM((tm, qk), F32),
                        pltpu.VMEM((tm, vd), F32), pltpu.VMEM((tm, vd), F32),
                        pltpu.VMEM((tm, qk), F32), pltpu.VMEM((tm, vd), F32),
                        pltpu.VMEM((heads, dv, dk), F32)],
        compiler_params=pltpu.CompilerParams(
            dimension_semantics=("arbitrary", "arbitrary"), vmem_limit_bytes=VMEM_LIMIT_BYTES),
        name="gla_layer",
    )(h, s0, p["w_in"], p["w_gate_up"], p["b_gate"], ng, p["w_out"], lng, lnb, router_p)


def _gla_params(w_in, w_gate_up, b_gate, norm_g, w_out):
    rank, qk = w_gate_up.shape
    zp = -(-rank // LANES) * LANES
    w_in_p = jnp.pad(w_in, ((0, 0), (0, zp - rank))).astype(BF16)
    wgu = jnp.pad(w_gate_up, ((0, zp - rank), (0, 0))).astype(BF16)
    return {"w_in": w_in_p, "w_gate_up": wgu, "b_gate": b_gate[None], "norm_g": norm_g,
            "w_out": w_out.astype(BF16)}


def _ffn_kernel(x_ref, win_ref, wout_ref, lng_ref, lnb_ref, out_ref, *, tf, alpha):
    x = x_ref[...]
    xb = x.astype(BF16)
    dff = wout_ref.shape[0]
    acc = jnp.zeros(x.shape, F32)
    for f in range(dff // tf):
        g = jnp.dot(xb, win_ref[:, f * tf:(f + 1) * tf], preferred_element_type=F32)
        u = jnp.dot(xb, win_ref[:, dff + f * tf:dff + (f + 1) * tf], preferred_element_type=F32)
        a = ((g * _sigmoid(g)) * u).astype(BF16)
        acc = acc + jnp.dot(a, wout_ref[f * tf:(f + 1) * tf, :], preferred_element_type=F32)
    out_ref[...] = _layer_norm(alpha * x + acc, lng_ref[...], lnb_ref[...])


def _ffn_layer(x, w_in, w_out, lng, lnb, *, tm, tf, alpha):
    rows, d = x.shape
    kern = functools.partial(_ffn_kernel, tf=tf, alpha=alpha)
    row_spec = pl.BlockSpec((tm, d), lambda i: (i, 0))
    return pl.pallas_call(
        kern,
        out_shape=jax.ShapeDtypeStruct((rows, d), F32),
        grid=(rows // tm,),
        in_specs=[row_spec,
                  pl.BlockSpec(w_in.shape, lambda i: (0, 0), pipeline_mode=pl.Buffered(1)),
                  pl.BlockSpec(w_out.shape, lambda i: (0, 0), pipeline_mode=pl.Buffered(1)),
                  _const_spec(lng.shape), _const_spec(lnb.shape)],
        out_specs=row_spec,
        compiler_params=pltpu.CompilerParams(
            dimension_semantics=("arbitrary",), vmem_limit_bytes=VMEM_LIMIT_BYTES),
        name="ffn_layer",
    )(x, w_in, w_out, lng, lnb)


def _moe_kernel(te_ref, tc_ref, tok0_ref, tokn_ref, dstp_ref, x_hbm, wg_ref, wu_ref, wo_ref, y_hbm,
                xbuf, xb, acc, obuf, sem_g, sem_s, *, tm, nf, npairs):
    j = pl.program_id(0)
    f = pl.program_id(1)
    chunk = tm // nf
    slot = j % 2
    other = 1 - slot
    active = tc_ref[j] > 0
    prev_active = tc_ref[jnp.maximum(j - 1, 0)] > 0
    drain = jnp.logical_and(jnp.logical_not(active), jnp.logical_and(j > 0, prev_active))

    def gather(tok_ref, r, s):
        return pltpu.make_async_copy(x_hbm.at[pl.ds(tok_ref[0, 0, r], 1), :],
                                     xbuf.at[s, pl.ds(r, 1), :], sem_g.at[s])

    def scatter(r, s):
        return pltpu.make_async_copy(obuf.at[s, pl.ds(r, 1), :],
                                     y_hbm.at[pl.ds(dstp_ref[0, 0, r], 1), :], sem_s.at[s])

    def wait_gathers(s):
        pltpu.make_async_copy(x_hbm.at[pl.ds(0, tm), :], xbuf.at[s], sem_g.at[s]).wait()

    def wait_scatters(s):
        pltpu.make_async_copy(obuf.at[s], y_hbm.at[pl.ds(0, tm), :], sem_s.at[s]).wait()

    @pl.when(jnp.logical_and(j == 0, f == 0))
    def _():
        obuf[...] = jnp.zeros(obuf.shape, obuf.dtype)
        for s in range(2):
            init = pltpu.make_async_copy(obuf.at[s], y_hbm.at[pl.ds(npairs + s * tm, tm), :], sem_s.at[s])
            init.start()
            init.wait()

        def start(r, c):
            gather(tok0_ref, r, 0).start()
            return c
        lax.fori_loop(0, tm, start, 0)

    @pl.when(jnp.logical_and(active, f == 0))
    def _():
        wait_gathers(slot)
        xb[...] = xbuf[slot].astype(BF16)

    @pl.when(active)
    def _():
        base = f * chunk
        for k in range(chunk):
            gather(tokn_ref, base + k, other).start()
            scatter(base + k, other).start()
        xv = xb[...]
        g = jnp.dot(xv, wg_ref[...], preferred_element_type=F32)
        u = jnp.dot(xv, wu_ref[...], preferred_element_type=F32)
        a = ((g * _sigmoid(g)) * u).astype(BF16)
        part = jnp.dot(a, wo_ref[...], preferred_element_type=F32)

        if nf > 1:
            @pl.when(f == 0)
            def _():
                acc[...] = part

            @pl.when(jnp.logical_and(f > 0, f < nf - 1))
            def _():
                acc[...] = acc[...] + part

        @pl.when(f == nf - 1)
        def _():
            @pl.when(j > 0)
            def _():
                wait_scatters(slot)
            obuf[slot] = acc[...] + part if nf > 1 else part

    @pl.when(jnp.logical_and(drain, f == 0))
    def _():
        wait_gathers(slot)
        wait_scatters(slot)

        def start(r, c):
            scatter(r, other).start()
            return c
        lax.fori_loop(0, tm, start, 0)
        wait_scatters(other)


def _moe_experts(h1, tok_ids, dst_ids, tile_e, tile_cnt, w_in, w_out, *, tm, tf):
    n, d = h1.shape
    dff = w_out.shape[1]
    nf = dff // tf
    nt = tok_ids.shape[0]
    kern = functools.partial(_moe_kernel, tm=tm, nf=nf, npairs=TOP_K * n)

    def f_eff(j, f, tc):
        return jnp.where(tc[j] > 0, f, nf - 1)

    smem_tile = functools.partial(pl.BlockSpec, (1, 1, tm), memory_space=pltpu.SMEM)
    grid_spec = pltpu.PrefetchScalarGridSpec(
        num_scalar_prefetch=2,
        grid=(nt, nf),
        in_specs=[
            smem_tile(lambda j, f, te, tc: (0, 0, 0)),
            smem_tile(lambda j, f, te, tc: (jnp.minimum(j + 1, nt - 1), 0, 0)),
            smem_tile(lambda j, f, te, tc: (j, 0, 0)),
            pl.BlockSpec(memory_space=pl.ANY),
            pl.BlockSpec((None, d, tf), lambda j, f, te, tc: (te[j], 0, f_eff(j, f, tc))),
            pl.BlockSpec((None, d, tf), lambda j, f, te, tc: (te[j], 0, nf + f_eff(j, f, tc))),
            pl.BlockSpec((None, tf, d), lambda j, f, te, tc: (te[j], f_eff(j, f, tc), 0)),
        ],
        out_specs=pl.BlockSpec(memory_space=pl.ANY),
        scratch_shapes=[pltpu.VMEM((2, tm, d), F32), pltpu.VMEM((tm, d), BF16),
                        pltpu.VMEM((tm, d), F32), pltpu.VMEM((2, tm, d), F32),
                        pltpu.SemaphoreType.DMA((2,)), pltpu.SemaphoreType.DMA((2,))],
    )
    return pl.pallas_call(
        kern,
        out_shape=jax.ShapeDtypeStruct((TOP_K * n + 2 * tm, d), F32),
        grid_spec=grid_spec,
        compiler_params=pltpu.CompilerParams(
            dimension_semantics=("arbitrary", "arbitrary"), vmem_limit_bytes=VMEM_LIMIT_BYTES),
        name="moe_experts",
    )(tile_e, tile_cnt, tok_ids, tok_ids, dst_ids, h1, w_in, w_in, w_out)


def _moe_schedule(route, n_experts, tm):
    n = route.shape[0]
    npairs = TOP_K * n
    e_flat = jnp.concatenate([route[:, 2], route[:, 3]]).astype(jnp.int32)
    onehot = (e_flat[:, None] == jnp.arange(n_experts, dtype=jnp.int32)[None, :]).astype(jnp.int32)
    csum = jnp.cumsum(onehot, axis=0)
    counts = csum[-1]
    rank = jnp.sum(onehot * csum, axis=1) - 1
    ustart = jnp.cumsum(counts) - counts
    pos = ustart[e_flat] + rank
    order = jnp.zeros((npairs,), jnp.int32).at[pos].set(
        jnp.arange(npairs, dtype=jnp.int32), unique_indices=True)
    padded = -(-counts // tm) * tm
    pend = jnp.cumsum(padded)
    pstart = pend - padded
    nt = -(-npairs // tm) + n_experts
    tile_start = jnp.arange(nt, dtype=jnp.int32) * tm
    tile_e = jnp.minimum(jnp.sum(tile_start[:, None] >= pend[None, :], axis=1), n_experts - 1).astype(jnp.int32)
    in_group = tile_start - pstart[tile_e]
    tile_cnt = jnp.where(tile_start < pend[-1], jnp.clip(counts[tile_e] - in_group, 0, tm), 0).astype(jnp.int32)
    r = jnp.arange(tm, dtype=jnp.int32)[None, :]
    src = (ustart[tile_e] + in_group)[:, None] + r
    pair = order[jnp.clip(src, 0, npairs - 1)]
    tok_ids = pair % n
    parity = (jnp.arange(nt, dtype=jnp.int32) % 2)[:, None]
    dst = jnp.where(r < tile_cnt[:, None], pair, npairs + parity * tm + r)
    dst_ids = jnp.concatenate([npairs + tm + r, dst], axis=0)
    return tok_ids.reshape(nt, 1, tm), dst_ids.reshape(nt + 1, 1, tm), tile_e, tile_cnt


def _combine_kernel(h_ref, y0_ref, y1_ref, route_ref, lng_ref, lnb_ref, out_ref, *, alpha):
    route = route_ref[...]
    ff = route[:, 0:1] * y0_ref[...] + route[:, 1:2] * y1_ref[...]
    out_ref[...] = _layer_norm(alpha * h_ref[...] + ff, lng_ref[...], lnb_ref[...])


def _moe_combine(h1, y2, route, lng, lnb, *, tm, alpha):
    n, d = h1.shape
    nt = n // tm
    kern = functools.partial(_combine_kernel, alpha=alpha)
    row_spec = pl.BlockSpec((tm, d), lambda i: (i, 0))
    return pl.pallas_call(
        kern,
        out_shape=jax.ShapeDtypeStruct((n, d), F32),
        grid=(nt,),
        in_specs=[row_spec, row_spec, pl.BlockSpec((tm, d), lambda i: (nt + i, 0)),
                  pl.BlockSpec((tm, LANES), lambda i: (i, 0)),
                  _const_spec(lng.shape), _const_spec(lnb.shape)],
        out_specs=row_spec,
        compiler_params=pltpu.CompilerParams(
            dimension_semantics=("arbitrary",), vmem_limit_bytes=VMEM_LIMIT_BYTES),
        name="moe_combine",
    )(h1, y2, y2, route, lng, lnb)


def _moe_layer(h1, route, w_in, w_out, lng, lnb, *, tm_moe, tf, tm_row, alpha):
    n_experts = w_in.shape[0]
    tok_ids, dst_ids, tile_e, tile_cnt = _moe_schedule(route, n_experts, tm_moe)
    y2 = _moe_experts(h1, tok_ids, dst_ids, tile_e, tile_cnt, w_in, w_out, tm=tm_moe, tf=tf)
    return _moe_combine(h1, y2, route, lng, lnb, tm=tm_row, alpha=alpha)


def _row_tile(rows, pref):
    tm = min(pref, rows)
    while rows % tm:
        tm //= 2
    return tm


def kernel(x, meta_tokens, ln_gain, ln_bias, rg_w_in, rg_conv_w, rg_conv_b, rg_w_gates, rg_b_gates, rg_lambda, rg_w_out, gla_w_in, gla_w_gate_up, gla_b_gate, gla_norm_g, gla_w_out, ffn_w_in, ffn_w_out, moe_router, moe_w_in, moe_w_out):
    bsz, seq, d = x.shape
    n_meta = meta_tokens.shape[0]
    depth = ln_gain.shape[0]
    alpha = (2.0 * depth) ** 0.25
    n_experts = moe_router.shape[-1]
    heads, dv = gla_norm_g.shape[1:]
    dk = gla_w_gate_up.shape[-1] // heads

    hm = x.reshape(bsz * seq, d)
    hq = meta_tokens.astype(x.dtype)

    tm_rg = _row_tile(seq, 256)
    tm_gla = _row_tile(seq, 256)
    tm_ffn = _row_tile(bsz * seq, 512)
    tf = 512
    tm_moe = (moe_w_out.shape[2] // tf) * LANES
    tf_meta = moe_w_out.shape[2] // 2

    for i in range(depth):
        j = i // 2
        lng0, lnb0 = ln_gain[i, 0][None], ln_bias[i, 0][None]
        lng1, lnb1 = ln_gain[i, 1][None], ln_bias[i, 1][None]
        last = i == depth - 1
        if i % 2 == 0:
            p = _rg_params(rg_w_in[j], rg_conv_w[j], rg_conv_b[j], rg_w_gates[j], rg_b_gates[j],
                           rg_lambda[j], rg_w_out[j])
            wp = p["conv_b"].shape[1]
            zero_state = jnp.zeros((SUBLANES, wp), F32)
            hq, tail, hlast = _rg_layer(hq, zero_state, zero_state, p, lng0, lnb0,
                                        nb=1, tm=n_meta, alpha=alpha)
            hm, _, _ = _rg_layer(hm, tail, hlast, p, lng0, lnb0, nb=bsz, tm=tm_rg, alpha=alpha)
            w_in = ffn_w_in[j].astype(BF16)
            w_out = ffn_w_out[j].astype(BF16)
            if not last:
                hq = _ffn_layer(hq, w_in, w_out, lng1, lnb1, tm=n_meta, tf=tf, alpha=alpha)
            hm = _ffn_layer(hm, w_in, w_out, lng1, lnb1, tm=tm_ffn, tf=tf, alpha=alpha)
        else:
            p = _gla_params(gla_w_in[j], gla_w_gate_up[j], gla_b_gate[j], gla_norm_g[j], gla_w_out[j])
            router_p = jnp.pad(moe_router[j], ((0, 0), (0, LANES - n_experts)))
            s0 = jnp.zeros((heads, dv, dk), F32)
            hq, route_q, s_meta = _gla_layer(hq, s0, p, lng0, lnb0, router_p, nb=1, tm=n_meta,
                                             chunk=n_meta, alpha=alpha, n_experts=n_experts)
            hm, route_m, _ = _gla_layer(hm, s_meta, p, lng0, lnb0, router_p, nb=bsz, tm=tm_gla,
                                        chunk=GLA_CHUNK, alpha=alpha, n_experts=n_experts)
            w_in = moe_w_in[j].astype(BF16)
            w_out = moe_w_out[j].astype(BF16)
            if not last:
                hq = _moe_layer(hq, route_q, w_in, w_out, lng1, lnb1, tm_moe=n_meta, tf=tf_meta,
                                tm_row=n_meta, alpha=alpha)
            hm = _moe_layer(hm, route_m, w_in, w_out, lng1, lnb1, tm_moe=tm_moe, tf=tf,
                            tm_row=tm_ffn, alpha=alpha)
    return hm.reshape(bsz, seq, d)
```

```python
import functools

import jax
import jax.numpy as jnp
from jax import lax
from jax.experimental import pallas as pl
from jax.experimental.pallas import tpu as pltpu

F32 = jnp.float32
BF16 = jnp.bfloat16

LANES = 128
SUBLANES = 8
VMEM_LIMIT_BYTES = 56 * 1024 * 1024

LN_EPS = 1e-5
RG_C = 8.0
RG_CONV_W = 4
GLA_TAU = 16.0
GLA_CHUNK = 64
TOP_K = 2
GELU_C0 = 0.7978845608028654
GELU_C1 = 0.044715


def _layer_norm(z, g, b):
    mu = jnp.mean(z, axis=-1, keepdims=True)
    zc = z - mu
    var = jnp.mean(zc * zc, axis=-1, keepdims=True)
    return zc * lax.rsqrt(var + LN_EPS) * g + b


def _sigmoid(x):
    return 1.0 / (1.0 + jnp.exp(-x))


def _softplus(x):
    return jnp.maximum(x, 0.0) + jnp.log1p(jnp.exp(-jnp.abs(x)))


def _gelu_tanh(x):
    return 0.5 * x * (1.0 + jnp.tanh(GELU_C0 * (x + GELU_C1 * (x * x * x))))


def _const_spec(shape):
    nd = len(shape)
    return pl.BlockSpec(shape, lambda *_: (0,) * nd)


def _resident_spec(shape):
    nd = len(shape)
    return pl.BlockSpec(shape, lambda *_: (0,) * nd, pipeline_mode=pl.Buffered(1))


def _rg_kernel(h_ref, tail0_ref, h0_ref, win_ref, cw_ref, cb_ref, wg_ref, bg_ref,
               lam_ref, wout_ref, lng_ref, lnb_ref,
               out_ref, tail_out_ref, hlast_out_ref,
               xr_scr, gate_scr, xc_scr, rp_scr, ip_scr, y_scr, carry_scr,
               *, tm, wp, ngroups, alpha):
    i = pl.program_id(1)
    gw = wp // ngroups

    @pl.when(i == 0)
    def _():
        xr_scr[0:SUBLANES, :] = tail0_ref[...]
        carry_scr[...] = h0_ref[...]

    @pl.when(i > 0)
    def _():
        xr_scr[0:SUBLANES, :] = xr_scr[tm:tm + SUBLANES, :]

    hb = h_ref[...].astype(BF16)
    gate_scr[...] = jnp.dot(hb, win_ref[:, :wp], preferred_element_type=F32)
    xr_scr[SUBLANES:SUBLANES + tm, :] = jnp.dot(hb, win_ref[:, wp:], preferred_element_type=F32)

    xc = cb_ref[...]
    for tap in range(RG_CONV_W):
        off = SUBLANES - (RG_CONV_W - 1) + tap
        xc = xc + xr_scr[off:off + tm, :] * cw_ref[tap:tap + 1, :]
    xc_scr[...] = xc
    xcb = xc.astype(BF16)
    for g in range(ngroups):
        pre = jnp.dot(xcb[:, g * gw:(g + 1) * gw], wg_ref[g], preferred_element_type=F32)
        rp_scr[:, g * gw:(g + 1) * gw] = pre[:, :gw]
        ip_scr[:, g * gw:(g + 1) * gw] = pre[:, gw:]

    cvec = -RG_C * _softplus(-lam_ref[...])
    row = lax.broadcasted_iota(jnp.int32, (SUBLANES, LANES), 0)

    def body(j, carry):
        r0 = pl.multiple_of(j * SUBLANES, SUBLANES)
        rows = pl.ds(r0, SUBLANES)
        outs = []
        for c in range(wp // LANES):
            cs = slice(c * LANES, (c + 1) * LANES)
            r_t = _sigmoid(rp_scr[rows, cs] + bg_ref[0:1, cs])
            i_t = _sigmoid(ip_scr[rows, cs] + bg_ref[1:2, cs])
            log_a = r_t * cvec[:, cs]
            a_t = jnp.exp(log_a)
            u_t = jnp.sqrt(1.0 - a_t * a_t) * (i_t * xc_scr[rows, cs])
            av, bv = a_t, u_t
            for d in (1, 2, 4):
                m = row >= d
                a_sh = jnp.where(m, pltpu.roll(av, d, 0), 1.0)
                b_sh = jnp.where(m, pltpu.roll(bv, d, 0), 0.0)
                bv = av * b_sh + bv
                av = av * a_sh
            hs = av * carry[:, cs] + bv
            y_scr[rows, cs] = hs * _gelu_tanh(gate_scr[rows, cs])
            outs.append(jnp.broadcast_to(hs[SUBLANES - 1:SUBLANES, :], (SUBLANES, LANES)))
        return jnp.concatenate(outs, axis=1)

    carry = lax.fori_loop(0, tm // SUBLANES, body, carry_scr[...])
    carry_scr[...] = carry

    mix = jnp.dot(y_scr[...].astype(BF16), wout_ref[...], preferred_element_type=F32)
    out_ref[...] = _layer_norm(alpha * h_ref[...] + mix, lng_ref[...], lnb_ref[...])
    tail_out_ref[...] = xr_scr[tm:tm + SUBLANES, :]
    hlast_out_ref[...] = carry


def _rg_layer(h, tail0, h0, p, lng, lnb, *, nb, tm, alpha):
    rows, d = h.shape
    t = rows // nb
    nt = t // tm
    wp = p["conv_b"].shape[1]
    ngroups = p["w_gates"].shape[0]
    kern = functools.partial(_rg_kernel, tm=tm, wp=wp, ngroups=ngroups, alpha=alpha)
    row_spec = pl.BlockSpec((tm, d), lambda b, i: (b * nt + i, 0))
    state_spec = pl.BlockSpec((SUBLANES, wp), lambda b, i: (b, 0))
    return pl.pallas_call(
        kern,
        out_shape=(jax.ShapeDtypeStruct((rows, d), F32),
                   jax.ShapeDtypeStruct((nb * SUBLANES, wp), F32),
                   jax.ShapeDtypeStruct((nb * SUBLANES, wp), F32)),
        grid=(nb, nt),
        in_specs=[row_spec, _const_spec((SUBLANES, wp)), _const_spec((SUBLANES, wp)),
                  _resident_spec(p["w_in"].shape), _const_spec(p["conv_w"].shape),
                  _const_spec(p["conv_b"].shape), _resident_spec(p["w_gates"].shape),
                  _const_spec(p["b_gates"].shape), _const_spec(p["lam"].shape),
                  _resident_spec(p["w_out"].shape), _const_spec(lng.shape), _const_spec(lnb.shape)],
        out_specs=(row_spec, state_spec, state_spec),
        scratch_shapes=[pltpu.VMEM((tm + SUBLANES, wp), F32)] + [pltpu.VMEM((tm, wp), F32)] * 5
                       + [pltpu.VMEM((SUBLANES, wp), F32)],
        compiler_params=pltpu.CompilerParams(
            dimension_semantics=("arbitrary", "arbitrary"), vmem_limit_bytes=VMEM_LIMIT_BYTES),
        name="rg_layer",
    )(h, tail0, h0, p["w_in"], p["conv_w"], p["conv_b"], p["w_gates"], p["b_gates"],
      p["lam"], p["w_out"], lng, lnb)


def _rg_params(w_in, conv_w, conv_b, w_gates, b_gates, lam, w_out):
    width = conv_b.shape[0]
    nblk, bw = w_gates.shape[1], w_gates.shape[2]
    ngroups = nblk // 2
    gw_real = 2 * bw
    gw = -(-gw_real // LANES) * LANES

    def pad_ch(a):
        lead = a.shape[:-1]
        a = a.reshape(lead + (ngroups, gw_real))
        a = jnp.pad(a, [(0, 0)] * len(lead) + [(0, 0), (0, gw - gw_real)])
        return a.reshape(lead + (ngroups * gw,))

    w_in_p = jnp.concatenate([pad_ch(w_in[:, :width]), pad_ch(w_in[:, width:])], axis=1).astype(BF16)
    wg4 = w_gates.reshape(2, ngroups, 2, bw, bw)
    zeros = jnp.zeros((2, ngroups, bw, bw), F32)
    top = jnp.concatenate([wg4[:, :, 0], zeros], axis=-1)
    bot = jnp.concatenate([zeros, wg4[:, :, 1]], axis=-1)
    bd = jnp.concatenate([top, bot], axis=-2)
    bd = jnp.pad(bd, [(0, 0), (0, 0), (0, gw - gw_real), (0, gw - gw_real)])
    w_g = jnp.concatenate([bd[0], bd[1]], axis=-1).astype(BF16)
    w_out_p = pad_ch(w_out.T).T.astype(BF16)
    return {"w_in": w_in_p, "conv_w": pad_ch(conv_w), "conv_b": pad_ch(conv_b)[None],
            "w_gates": w_g, "b_gates": pad_ch(b_gates), "lam": pad_ch(lam)[None], "w_out": w_out_p}


def _gla_kernel(h_ref, s0_ref, win_ref, wgu_ref, bgate_ref, ng_ref, wout_ref, lng_ref, lnb_ref,
                router_ref,
                out_ref, route_ref, s_out_ref,
                q_scr, k_scr, v_scr, r_scr, b_scr, o_scr, st_scr,
                *, tm, chunk, heads, dk, dv, zp, n_experts, alpha):
    i = pl.program_id(1)
    qk = heads * dk
    vd = heads * dv

    @pl.when(i == 0)
    def _():
        st_scr[...] = s0_ref[...]

    hb = h_ref[...].astype(BF16)
    q_scr[...] = jnp.dot(hb, win_ref[:, 0:qk], preferred_element_type=F32) * (dk ** -0.5)
    k_scr[...] = jnp.dot(hb, win_ref[:, qk:2 * qk], preferred_element_type=F32)
    v_scr[...] = jnp.dot(hb, win_ref[:, 2 * qk:2 * qk + vd], preferred_element_type=F32).astype(BF16)
    r_scr[...] = jnp.dot(hb, win_ref[:, 2 * qk + vd:2 * qk + 2 * vd], preferred_element_type=F32)
    z = jnp.dot(hb, win_ref[:, 2 * qk + 2 * vd:2 * qk + 2 * vd + zp], preferred_element_type=F32)
    pre = jnp.dot(z.astype(BF16), wgu_ref[...], preferred_element_type=F32) + bgate_ref[...]
    lg = (jnp.minimum(pre, 0.0) - jnp.log1p(jnp.exp(-jnp.abs(pre)))) * (1.0 / GLA_TAU)

    shift = chunk.bit_length() - 1
    ti = lax.broadcasted_iota(jnp.int32, (tm, tm), 0)
    tj = lax.broadcasted_iota(jnp.int32, (tm, tm), 1)
    same_chunk = lax.shift_right_logical(ti, shift) == lax.shift_right_logical(tj, shift)
    tri_bd = jnp.where(jnp.logical_and(same_chunk, ti >= tj), 1.0, 0.0).astype(BF16)
    lg_hi = lg.astype(BF16)
    lg_lo = (lg - lg_hi.astype(F32)).astype(BF16)
    b_scr[...] = (jnp.dot(tri_bd, lg_hi, preferred_element_type=F32)
                  + jnp.dot(tri_bd, lg_lo, preferred_element_type=F32))

    ri = lax.broadcasted_iota(jnp.int32, (chunk, chunk), 0)
    ci = lax.broadcasted_iota(jnp.int32, (chunk, chunk), 1)
    causal = ri >= ci
    mid = chunk // 2 - 1

    def chunk_body(c, _):
        r0 = pl.multiple_of(c * chunk, chunk)
        rows = pl.ds(r0, chunk)
        for hd in range(heads):
            ks = slice(hd * dk, (hd + 1) * dk)
            vs = slice(hd * dv, (hd + 1) * dv)
            bcum = b_scr[rows, ks]
            b_mid = bcum[mid:mid + 1, :]
            b_last = bcum[chunk - 1:chunk, :]
            qv = q_scr[rows, ks]
            kv = k_scr[rows, ks]
            vv = v_scr[rows, vs]
            q_mid = (qv * jnp.exp(bcum - b_mid)).astype(BF16)
            k_mid = (kv * jnp.exp(b_mid - bcum)).astype(BF16)
            scores = lax.dot_general(q_mid, k_mid, (((1,), (1,)), ((), ())),
                                     preferred_element_type=F32)
            scores = jnp.where(causal, scores, 0.0).astype(BF16)
            o_intra = jnp.dot(scores, vv, preferred_element_type=F32)
            st = st_scr[hd]
            q_in = (qv * jnp.exp(bcum)).astype(BF16)
            o_inter = lax.dot_general(q_in, st.astype(BF16), (((1,), (1,)), ((), ())),
                                      preferred_element_type=F32)
            o_scr[rows, vs] = o_intra + o_inter
            k_out = (kv * jnp.exp(b_last - bcum)).astype(BF16)
            upd = lax.dot_general(vv, k_out, (((0,), (0,)), ((), ())),
                                  preferred_element_type=F32)
            st_scr[hd] = st * jnp.exp(b_last) + upd
        return 0

    lax.fori_loop(0, tm // chunk, chunk_body, 0)

    mix = jnp.zeros((tm, out_ref.shape[1]), F32)
    for hd in range(heads):
        vs = slice(hd * dv, (hd + 1) * dv)
        o = o_scr[:, vs]
        mu = jnp.mean(o, axis=-1, keepdims=True)
        oc = o - mu
        var = jnp.mean(oc * oc, axis=-1, keepdims=True)
        on = oc * lax.rsqrt(var + LN_EPS) * ng_ref[:, vs]
        rv = r_scr[:, vs]
        gated = (on * (rv * _sigmoid(rv))).astype(BF16)
        mix = mix + jnp.dot(gated, wout_ref[vs, :], preferred_element_type=F32)
    h1 = _layer_norm(alpha * h_ref[...] + mix, lng_ref[...], lnb_ref[...])
    out_ref[...] = h1
    s_out_ref[...] = st_scr[...]

    logits = jnp.dot(h1.astype(BF16), router_ref[...], preferred_element_type=F32)
    lane = lax.broadcasted_iota(jnp.int32, logits.shape, 1)
    neg = jnp.float32(-jnp.inf)
    logits = jnp.where(lane < n_experts, logits, neg)
    top1 = jnp.max(logits, axis=-1, keepdims=True)
    idx1 = jnp.min(jnp.where(logits == top1, lane, LANES), axis=-1, keepdims=True)
    rest = jnp.where(lane == idx1, neg, logits)
    top2 = jnp.max(rest, axis=-1, keepdims=True)
    idx2 = jnp.min(jnp.where(rest == top2, lane, LANES), axis=-1, keepdims=True)
    e2 = jnp.exp(top2 - top1)
    w1 = 1.0 / (1.0 + e2)
    w2 = e2 / (1.0 + e2)
    route = jnp.where(lane == 0, w1, 0.0)
    route = jnp.where(lane == 1, w2, route)
    route = jnp.where(lane == 2, idx1.astype(F32), route)
    route = jnp.where(lane == 3, idx2.astype(F32), route)
    route_ref[...] = route


def _gla_layer(h, s0, p, lng, lnb, router_p, *, nb, tm, chunk, alpha, n_experts):
    rows, d = h.shape
    t = rows // nb
    nt = t // tm
    heads, dv = p["norm_g"].shape
    qk = p["w_gate_up"].shape[1]
    dk = qk // heads
    vd = heads * dv
    zp = p["w_gate_up"].shape[0]
    kern = functools.partial(_gla_kernel, tm=tm, chunk=chunk, heads=heads, dk=dk, dv=dv, zp=zp,
                             n_experts=n_experts, alpha=alpha)
    row_spec = pl.BlockSpec((tm, d), lambda b, i: (b * nt + i, 0))
    ng = p["norm_g"].reshape(1, vd)
    return pl.pallas_call(
        kern,
        out_shape=(jax.ShapeDtypeStruct((rows, d), F32),
                   jax.ShapeDtypeStruct((rows, LANES), F32),
                   jax.ShapeDtypeStruct((nb * heads, dv, dk), F32)),
        grid=(nb, nt),
        in_specs=[row_spec, _const_spec((heads, dv, dk)), _resident_spec(p["w_in"].shape),
                  _const_spec(p["w_gate_up"].shape), _const_spec(p["b_gate"].shape),
                  _const_spec(ng.shape), _resident_spec(p["w_out"].shape),
                  _const_spec(lng.shape), _const_spec(lnb.shape), _const_spec(router_p.shape)],
        out_specs=(row_spec,
                   pl.BlockSpec((tm, LANES), lambda b, i: (b * nt + i, 0)),
                   pl.BlockSpec((heads, dv, dk), lambda b, i: (b, 0, 0))),
        scratch_shapes=[pltpu.VMEM((tm, qk), F32), pltpu.VMEM((tm, qk), F32),
                        pltpu.VMEM((tm, vd), BF16), pltpu.VMEM((tm, vd), F32),
                        pltpu.VMEM((tm, qk), F32), pltpu.VMEM((tm, vd), F32),
                        pltpu.VMEM((heads, dv, dk), F32)],
        compiler_params=pltpu.CompilerParams(
            dimension_semantics=("arbitrary", "arbitrary"), vmem_limit_bytes=VMEM_LIMIT_BYTES),
        name="gla_layer",
    )(h, s0, p["w_in"], p["w_gate_up"], p["b_gate"], ng, p["w_out"], lng, lnb, router_p)


def _gla_params(w_in, w_gate_up, b_gate, norm_g, w_out):
    rank, qk = w_gate_up.shape
    zp = -(-rank // LANES) * LANES
    w_in_p = jnp.pad(w_in, ((0, 0), (0, zp - rank))).astype(BF16)
    wgu = jnp.pad(w_gate_up, ((0, zp - rank), (0, 0))).astype(BF16)
    return {"w_in": w_in_p, "w_gate_up": wgu, "b_gate": b_gate[None], "norm_g": norm_g,
            "w_out": w_out.astype(BF16)}


def _ffn_kernel(x_ref, win_ref, wout_ref, lng_ref, lnb_ref, out_ref, *, tf, alpha):
    x = x_ref[...]
    xb = x.astype(BF16)
    dff = wout_ref.shape[0]
    acc = jnp.zeros(x.shape, F32)
    for f in range(dff // tf):
        g = jnp.dot(xb, win_ref[:, f * tf:(f + 1) * tf], preferred_element_type=F32)
        u = jnp.dot(xb, win_ref[:, dff + f * tf:dff + (f + 1) * tf], preferred_element_type=F32)
        a = ((g * _sigmoid(g)) * u).astype(BF16)
        acc = acc + jnp.dot(a, wout_ref[f * tf:(f + 1) * tf, :], preferred_element_type=F32)
    out_ref[...] = _layer_norm(alpha * x + acc, lng_ref[...], lnb_ref[...])


def _ffn_layer(x, w_in, w_out, lng, lnb, *, tm, tf, alpha):
    rows, d = x.shape
    kern = functools.partial(_ffn_kernel, tf=tf, alpha=alpha)
    row_spec = pl.BlockSpec((tm, d), lambda i: (i, 0))
    return pl.pallas_call(
        kern,
        out_shape=jax.ShapeDtypeStruct((rows, d), F32),
        grid=(rows // tm,),
        in_specs=[row_spec, _resident_spec(w_in.shape), _resident_spec(w_out.shape),
                  _const_spec(lng.shape), _const_spec(lnb.shape)],
        out_specs=row_spec,
        compiler_params=pltpu.CompilerParams(
            dimension_semantics=("arbitrary",), vmem_limit_bytes=VMEM_LIMIT_BYTES),
        name="ffn_layer",
    )(x, w_in, w_out, lng, lnb)


def _moe_kernel(te_ref, tc_ref, tok0_ref, tokn_ref, dstp_ref, x_hbm, wg_ref, wu_ref, wo_ref, y_hbm,
                xbuf, xb, acc, obuf, sem_g, sem_s, *, tm, nf, npairs):
    j = pl.program_id(0)
    f = pl.program_id(1)
    chunk = tm // nf
    slot = j % 2
    other = 1 - slot
    active = tc_ref[j] > 0
    prev_active = tc_ref[jnp.maximum(j - 1, 0)] > 0
    drain = jnp.logical_and(jnp.logical_not(active), jnp.logical_and(j > 0, prev_active))

    def gather(tok_ref, r, s):
        return pltpu.make_async_copy(x_hbm.at[pl.ds(tok_ref[0, 0, r], 1), :],
                                     xbuf.at[s, pl.ds(r, 1), :], sem_g.at[s])

    def scatter(r, s):
        return pltpu.make_async_copy(obuf.at[s, pl.ds(r, 1), :],
                                     y_hbm.at[pl.ds(dstp_ref[0, 0, r], 1), :], sem_s.at[s])

    def wait_gathers(s):
        pltpu.make_async_copy(x_hbm.at[pl.ds(0, tm), :], xbuf.at[s], sem_g.at[s]).wait()

    def wait_scatters(s):
        pltpu.make_async_copy(obuf.at[s], y_hbm.at[pl.ds(0, tm), :], sem_s.at[s]).wait()

    @pl.when(jnp.logical_and(j == 0, f == 0))
    def _():
        obuf[...] = jnp.zeros(obuf.shape, obuf.dtype)
        for s in range(2):
            init = pltpu.make_async_copy(obuf.at[s], y_hbm.at[pl.ds(npairs + s * tm, tm), :], sem_s.at[s])
            init.start()
            init.wait()

        def start(r, c):
            gather(tok0_ref, r, 0).start()
            return c
        lax.fori_loop(0, tm, start, 0)

    @pl.when(jnp.logical_and(active, f == 0))
    def _():
        wait_gathers(slot)
        xb[...] = xbuf[slot].astype(BF16)

    @pl.when(active)
    def _():
        base = f * chunk
        for k in range(chunk):
            gather(tokn_ref, base + k, other).start()
            scatter(base + k, other).start()
        xv = xb[...]
        g = jnp.dot(xv, wg_ref[...], preferred_element_type=F32)
        u = jnp.dot(xv, wu_ref[...], preferred_element_type=F32)
        a = ((g * _sigmoid(g)) * u).astype(BF16)
        part = jnp.dot(a, wo_ref[...], preferred_element_type=F32)

        if nf > 1:
            @pl.when(f == 0)
            def _():
                acc[...] = part

            @pl.when(jnp.logical_and(f > 0, f < nf - 1))
            def _():
                acc[...] = acc[...] + part

        @pl.when(f == nf - 1)
        def _():
            @pl.when(j > 0)
            def _():
                wait_scatters(slot)
            obuf[slot] = acc[...] + part if nf > 1 else part

    @pl.when(jnp.logical_and(drain, f == 0))
    def _():
        wait_gathers(slot)
        wait_scatters(slot)

        def start(r, c):
            scatter(r, other).start()
            return c
        lax.fori_loop(0, tm, start, 0)
        wait_scatters(other)


def _moe_experts(h1, tok_ids, dst_ids, tile_e, tile_cnt, w_in, w_out, *, tm, tf):
    n, d = h1.shape
    dff = w_out.shape[1]
    nf = dff // tf
    nt = tok_ids.shape[0]
    kern = functools.partial(_moe_kernel, tm=tm, nf=nf, npairs=TOP_K * n)

    def f_eff(j, f, tc):
        return jnp.where(tc[j] > 0, f, nf - 1)

    smem_tile = functools.partial(pl.BlockSpec, (1, 1, tm), memory_space=pltpu.SMEM)
    grid_spec = pltpu.PrefetchScalarGridSpec(
        num_scalar_prefetch=2,
        grid=(nt, nf),
        in_specs=[
            smem_tile(lambda j, f, te, tc: (0, 0, 0)),
            smem_tile(lambda j, f, te, tc: (jnp.minimum(j + 1, nt - 1), 0, 0)),
            smem_tile(lambda j, f, te, tc: (j, 0, 0)),
            pl.BlockSpec(memory_space=pl.ANY),
            pl.BlockSpec((None, d, tf), lambda j, f, te, tc: (te[j], 0, f_eff(j, f, tc))),
            pl.BlockSpec((None, d, tf), lambda j, f, te, tc: (te[j], 0, nf + f_eff(j, f, tc))),
            pl.BlockSpec((None, tf, d), lambda j, f, te, tc: (te[j], f_eff(j, f, tc), 0)),
        ],
        out_specs=pl.BlockSpec(memory_space=pl.ANY),
        scratch_shapes=[pltpu.VMEM((2, tm, d), F32), pltpu.VMEM((tm, d), BF16),
                        pltpu.VMEM((tm, d), F32), pltpu.VMEM((2, tm, d), F32),
                        pltpu.SemaphoreType.DMA((2,)), pltpu.SemaphoreType.DMA((2,))],
    )
    return pl.pallas_call(
        kern,
        out_shape=jax.ShapeDtypeStruct((TOP_K * n + 2 * tm, d), F32),
        grid_spec=grid_spec,
        compiler_params=pltpu.CompilerParams(
            dimension_semantics=("arbitrary", "arbitrary"), vmem_limit_bytes=VMEM_LIMIT_BYTES),
        name="moe_experts",
    )(tile_e, tile_cnt, tok_ids, tok_ids, dst_ids, h1, w_in, w_in, w_out)


def _moe_schedule(route, n_experts, tm):
    n = route.shape[0]
    npairs = TOP_K * n
    e_flat = jnp.concatenate([route[:, 2], route[:, 3]]).astype(jnp.int32)
    order = jnp.argsort(e_flat, stable=True).astype(jnp.int32)
    counts = jnp.sum(e_flat[:, None] == jnp.arange(n_experts, dtype=jnp.int32)[None, :],
                     axis=0, dtype=jnp.int32)
    ustart = jnp.cumsum(counts) - counts
    padded = -(-counts // tm) * tm
    pend = jnp.cumsum(padded)
    pstart = pend - padded
    nt = -(-npairs // tm) + n_experts
    tile_start = jnp.arange(nt, dtype=jnp.int32) * tm
    tile_e = jnp.minimum(jnp.sum(tile_start[:, None] >= pend[None, :], axis=1), n_experts - 1).astype(jnp.int32)
    in_group = tile_start - pstart[tile_e]
    tile_cnt = jnp.where(tile_start < pend[-1], jnp.clip(counts[tile_e] - in_group, 0, tm), 0).astype(jnp.int32)
    r = jnp.arange(tm, dtype=jnp.int32)[None, :]
    src = (ustart[tile_e] + in_group)[:, None] + r
    pair = order[jnp.clip(src, 0, npairs - 1)]
    tok_ids = pair % n
    parity = (jnp.arange(nt, dtype=jnp.int32) % 2)[:, None]
    dst = jnp.where(r < tile_cnt[:, None], pair, npairs + parity * tm + r)
    dst_ids = jnp.concatenate([npairs + tm + r, dst], axis=0)
    return tok_ids.reshape(nt, 1, tm), dst_ids.reshape(nt + 1, 1, tm), tile_e, tile_cnt


def _combine_kernel(h_ref, y0_ref, y1_ref, route_ref, lng_ref, lnb_ref, out_ref, *, alpha):
    route = route_ref[...]
    ff = route[:, 0:1] * y0_ref[...] + route[:, 1:2] * y1_ref[...]
    out_ref[...] = _layer_norm(alpha * h_ref[...] + ff, lng_ref[...], lnb_ref[...])


def _moe_combine(h1, y2, route, lng, lnb, *, tm, alpha):
    n, d = h1.shape
    nt = n // tm
    kern = functools.partial(_combine_kernel, alpha=alpha)
    row_spec = pl.BlockSpec((tm, d), lambda i: (i, 0))
    return pl.pallas_call(
        kern,
        out_shape=jax.ShapeDtypeStruct((n, d), F32),
        grid=(nt,),
        in_specs=[row_spec, row_spec, pl.BlockSpec((tm, d), lambda i: (nt + i, 0)),
                  pl.BlockSpec((tm, LANES), lambda i: (i, 0)),
                  _const_spec(lng.shape), _const_spec(lnb.shape)],
        out_specs=row_spec,
        compiler_params=pltpu.CompilerParams(
            dimension_semantics=("arbitrary",), vmem_limit_bytes=VMEM_LIMIT_BYTES),
        name="moe_combine",
    )(h1, y2, y2, route, lng, lnb)


def _moe_layer(h1, route, w_in, w_out, lng, lnb, *, tm_moe, tf, tm_row, alpha):
    n_experts = w_in.shape[0]
    tok_ids, dst_ids, tile_e, tile_cnt = _moe_schedule(route, n_experts, tm_moe)
    y2 = _moe_experts(h1, tok_ids, dst_ids, tile_e, tile_cnt, w_in, w_out, tm=tm_moe, tf=tf)
    return _moe_combine(h1, y2, route, lng, lnb, tm=tm_row, alpha=alpha)


def _row_tile(rows, pref):
    tm = min(pref, rows)
    while rows % tm:
        tm //= 2
    return tm


def kernel(x, meta_tokens, ln_gain, ln_bias, rg_w_in, rg_conv_w, rg_conv_b, rg_w_gates, rg_b_gates, rg_lambda, rg_w_out, gla_w_in, gla_w_gate_up, gla_b_gate, gla_norm_g, gla_w_out, ffn_w_in, ffn_w_out, moe_router, moe_w_in, moe_w_out):
    bsz, seq, d = x.shape
    n_meta = meta_tokens.shape[0]
    depth = ln_gain.shape[0]
    alpha = (2.0 * depth) ** 0.25
    n_experts = moe_router.shape[-1]
    heads, dv = gla_norm_g.shape[1:]
    dk = gla_w_gate_up.shape[-1] // heads

    hm = x.reshape(bsz * seq, d)
    hq = meta_tokens.astype(x.dtype)

    tm_rg = _row_tile(seq, 512)
    tm_gla = _row_tile(seq, 512)
    tm_ffn = _row_tile(bsz * seq, 512)
    tf = 512
    tm_moe = (moe_w_out.shape[2] // tf) * LANES
    tf_meta = moe_w_out.shape[2] // 2

    for i in range(depth):
        j = i // 2
        lng0, lnb0 = ln_gain[i, 0][None], ln_bias[i, 0][None]
        lng1, lnb1 = ln_gain[i, 1][None], ln_bias[i, 1][None]
        last = i == depth - 1
        if i % 2 == 0:
            p = _rg_params(rg_w_in[j], rg_conv_w[j], rg_conv_b[j], rg_w_gates[j], rg_b_gates[j],
                           rg_lambda[j], rg_w_out[j])
            wp = p["conv_b"].shape[1]
            zero_state = jnp.zeros((SUBLANES, wp), F32)
            hq, tail, hlast = _rg_layer(hq, zero_state, zero_state, p, lng0, lnb0,
                                        nb=1, tm=n_meta, alpha=alpha)
            hm, _, _ = _rg_layer(hm, tail, hlast, p, lng0, lnb0, nb=bsz, tm=tm_rg, alpha=alpha)
            w_in = ffn_w_in[j].astype(BF16)
            w_out = ffn_w_out[j].astype(BF16)
            if not last:
                hq = _ffn_layer(hq, w_in, w_out, lng1, lnb1, tm=n_meta, tf=tf, alpha=alpha)
            hm = _ffn_layer(hm, w_in, w_out, lng1, lnb1, tm=tm_ffn, tf=tf, alpha=alpha)
        else:
            p = _gla_params(gla_w_in[j], gla_w_gate_up[j], gla_b_gate[j], gla_norm_g[j], gla_w_out[j])
            router_p = jnp.pad(moe_router[j], ((0, 0), (0, LANES - n_experts))).astype(BF16)
            s0 = jnp.zeros((heads, dv, dk), F32)
            hq, route_q, s_meta = _gla_layer(hq, s0, p, lng0, lnb0, router_p, nb=1, tm=n_meta,
                                             chunk=n_meta, alpha=alpha, n_experts=n_experts)
            hm, route_m, _ = _gla_layer(hm, s_meta, p, lng0, lnb0, router_p, nb=bsz, tm=tm_gla,
                                        chunk=GLA_CHUNK, alpha=alpha, n_experts=n_experts)
            w_in = moe_w_in[j].astype(BF16)
            w_out = moe_w_out[j].astype(BF16)
            if not last:
                hq = _moe_layer(hq, route_q, w_in, w_out, lng1, lnb1, tm_moe=n_meta, tf=tf_meta,
                                tm_row=n_meta, alpha=alpha)
            hm = _moe_layer(hm, route_m, w_in, w_out, lng1, lnb1, tm_moe=tm_moe, tf=tf,
                            tm_row=tm_ffn, alpha=alpha)
    return hm.reshape(bsz, seq, d)
```
